```python
import math
import jax, jax.numpy as jnp
from jax import lax
import numpy as np

D_MODEL = 1024
BATCH = 8
SEQ = 4096
DEPTH = 2

HEAD_DIM = 64
FOX_HEADS = 8
SB_HEADS = 8
Q_BLOCK = 128
SSD_HEADS = 16
SSD_HEAD_DIM = 64
SSD_GROUPS = 2
SSD_STATE = 128
SSD_CONV = 4
SSD_CHUNK = 128
D_FOX = FOX_HEADS * HEAD_DIM
D_SB = SB_HEADS * HEAD_DIM
D_SSD = SSD_HEADS * SSD_HEAD_DIM
D_MIX = D_FOX + D_SB + D_SSD
D_BC = SSD_GROUPS * SSD_STATE
SSD_CONV_CH = D_SSD + 2 * D_BC
D_FF = ((8 * D_MODEL // 3 + 127) // 128) * 128
FFN_CONV = 3
NORM_EPS = 1e-6
IN_SIZES = [D_FOX, D_FOX, D_FOX, FOX_HEADS,
            D_SB, D_SB, D_SB,
            D_SSD, SSD_CONV_CH, SSD_HEADS]
N_IN = sum(IN_SIZES)
SPLIT_POINTS = np.cumsum(IN_SIZES)[:-1].tolist()

kernel_name = "hybrid_fox_ssd_stickbreak_block"


def rms_norm(x, g):
    xf = x.astype(jnp.float32)
    y = xf * lax.rsqrt(jnp.mean(xf * xf, axis=-1, keepdims=True) + NORM_EPS)
    return (y * g.astype(jnp.float32)).astype(x.dtype)


def grouped_rms_norm(x, g, n_groups):
    shp = x.shape
    xg = x.reshape(shp[:-1] + (n_groups, shp[-1] // n_groups))
    gg = g.reshape(n_groups, shp[-1] // n_groups)
    return rms_norm(xg, gg).reshape(shp)


def causal_depthwise_conv(x, w, b):
    width = w.shape[0]
    y = lax.conv_general_dilated(
        x, w[:, None, :].astype(x.dtype), window_strides=(1,), padding=[(width - 1, 0)],
        dimension_numbers=('NWC', 'WIO', 'NWC'), feature_group_count=x.shape[-1])
    return y + b.astype(x.dtype)


def forgetting_attention(q, k, v, log_f):
    b, s, h, dh = q.shape
    nb = s // Q_BLOCK
    q, k, v = (t.transpose(0, 2, 1, 3) for t in (q, k, v))
    c = jnp.cumsum(log_f, axis=1).transpose(0, 2, 1)
    qb = q.reshape(b, h, nb, Q_BLOCK, dh).transpose(2, 0, 1, 3, 4)
    cb = c.reshape(b, h, nb, Q_BLOCK).transpose(2, 0, 1, 3)
    pos = jnp.arange(s)
    qpos = pos.reshape(nb, Q_BLOCK)
    scale = dh ** -0.5

    def block(args):
        qi, ci, ti = args
        logits = jnp.einsum('bhqd,bhkd->bhqk', qi, k).astype(jnp.float32) * scale
        logits = logits + ci[..., :, None] - c[:, :, None, :]
        logits = jnp.where(ti[:, None] >= pos[None, :], logits, -jnp.inf)
        p = jax.nn.softmax(logits, axis=-1)
        return jnp.einsum('bhqk,bhkd->bhqd', p.astype(v.dtype), v)

    out = lax.map(block, (qb, cb, qpos))
    return out.transpose(1, 0, 3, 2, 4).reshape(b, s, h * dh)


def stick_breaking_attention(q, k, v):
    b, s, h, dh = q.shape
    nb = s // Q_BLOCK
    q, k, v = (t.transpose(0, 2, 1, 3) for t in (q, k, v))
    qb = q.reshape(b, h, nb, Q_BLOCK, dh).transpose(2, 0, 1, 3, 4)
    pos = jnp.arange(s)
    qpos = pos.reshape(nb, Q_BLOCK)
    scale = dh ** -0.5

    def block(args):
        qi, ti = args
        z = jnp.einsum('bhqd,bhkd->bhqk', qi, k).astype(jnp.float32) * scale
        mask = pos[None, :] < ti[:, None]
        log_keep = jnp.where(mask, jax.nn.log_sigmoid(-z), 0.0)
        cum = jnp.cumsum(log_keep, axis=-1)
        log_w = jax.nn.log_sigmoid(z) + cum[..., -1:] - cum
        w = jnp.where(mask, jnp.exp(log_w), 0.0)
        return jnp.einsum('bhqk,bhkd->bhqd', w.astype(v.dtype), v)

    out = lax.map(block, (qb, qpos))
    return out.transpose(1, 0, 3, 2, 4).reshape(b, s, h * dh)


def ssd_mixer(xbc, z, dt_raw, conv_w, conv_b, dt_bias, a_log, d_skip, norm_g):
    f32 = jnp.float32
    b, s, _ = xbc.shape
    hpg = SSD_HEADS // SSD_GROUPS
    nc = s // SSD_CHUNK
    xbc = jax.nn.silu(causal_depthwise_conv(xbc, conv_w, conv_b))
    xs, bm, cm = jnp.split(xbc, [D_SSD, D_SSD + D_BC], axis=-1)
    xs = xs.astype(f32).reshape(b, s, SSD_HEADS, SSD_HEAD_DIM)
    dt = jax.nn.softplus(dt_raw.astype(f32) + dt_bias.astype(f32))
    a = -jnp.exp(a_log.astype(f32))
    x_c = (xs * dt[..., None]).reshape(b, nc, SSD_CHUNK, SSD_GROUPS, hpg, SSD_HEAD_DIM)
    b_c = bm.astype(f32).reshape(b, nc, SSD_CHUNK, SSD_GROUPS, SSD_STATE)
    c_c = cm.astype(f32).reshape(b, nc, SSD_CHUNK, SSD_GROUPS, SSD_STATE)
    a_cs = jnp.cumsum((dt * a).reshape(b, nc, SSD_CHUNK, SSD_GROUPS, hpg), axis=2)
    a_t = a_cs.transpose(0, 1, 3, 4, 2)
    causal = jnp.tril(jnp.ones((SSD_CHUNK, SSD_CHUNK), dtype=bool))
    seg = jnp.exp(jnp.where(causal, a_t[..., :, None] - a_t[..., None, :], -jnp.inf))
    cb = jnp.einsum('bclgn,bcsgn->bcgls', c_c, b_c)
    y_diag = jnp.einsum('bcghls,bcsghp->bclghp', cb[:, :, :, None] * seg, x_c)
    decay_end = jnp.exp(a_cs[:, :, -1:] - a_cs)
    states = jnp.einsum('bclgn,bclghp->bcghpn', b_c, x_c * decay_end[..., None])
    chunk_decay = jnp.exp(a_cs[:, :, -1])

    def step(hstate, inp):
        dec, st = inp
        return dec[..., None, None] * hstate + st, hstate

    _, prev = lax.scan(step, jnp.zeros_like(states[:, 0]),
                       (jnp.moveaxis(chunk_decay, 1, 0), jnp.moveaxis(states, 1, 0)))
    prev = jnp.moveaxis(prev, 0, 1)
    y_off = jnp.einsum('bclgn,bcghpn->bclghp', c_c, prev) * jnp.exp(a_cs)[..., None]
    y = (y_diag + y_off).reshape(b, s, SSD_HEADS, SSD_HEAD_DIM) + xs * d_skip.astype(f32)[:, None]
    y = y.reshape(b, s, D_SSD) * jax.nn.silu(z.astype(f32))
    return grouped_rms_norm(y, norm_g, SSD_GROUPS).astype(z.dtype)


def hybrid_layer(x, mix_g, w_in, fox_f_bias, fox_out_g, sb_out_g, ssd_conv_w, ssd_conv_b,
                 ssd_dt_bias, ssd_a_log, ssd_d, ssd_norm_g, w_out, ffn_g, w_up,
                 ffn_conv_w, ffn_conv_b, w_down):
    b, s, _ = x.shape
    h = rms_norm(x, mix_g)
    proj = jnp.einsum('bsd,de->bse', h, w_in)
    fq, fk, fv, ff, sq, sk, sv, z, xbc, dt = jnp.split(proj, SPLIT_POINTS, axis=-1)
    heads = lambda t, n: t.reshape(b, s, n, HEAD_DIM)
    log_f = jax.nn.log_sigmoid(ff.astype(jnp.float32) + fox_f_bias.astype(jnp.float32))
    y_fox = grouped_rms_norm(
        forgetting_attention(heads(fq, FOX_HEADS), heads(fk, FOX_HEADS), heads(fv, FOX_HEADS), log_f),
        fox_out_g, FOX_HEADS)
    y_sb = grouped_rms_norm(
        stick_breaking_attention(heads(sq, SB_HEADS), heads(sk, SB_HEADS), heads(sv, SB_HEADS)),
        sb_out_g, SB_HEADS)
    y_ssd = ssd_mixer(xbc, z, dt, ssd_conv_w, ssd_conv_b, ssd_dt_bias, ssd_a_log, ssd_d, ssd_norm_g)
    y = jnp.concatenate([y_fox.astype(h.dtype), y_sb.astype(h.dtype), y_ssd], axis=-1)
    x = x + jnp.einsum('bse,ed->bsd', y, w_out)
    h = rms_norm(x, ffn_g)
    u = causal_depthwise_conv(jnp.einsum('bsd,df->bsf', h, w_up), ffn_conv_w, ffn_conv_b)
    gate, val = jnp.split(u, 2, axis=-1)
    return x + jnp.einsum('bsf,fd->bsd', jax.nn.silu(gate) * val, w_down)


def setup_inputs(seed: int = 0) -> dict:
    key = jax.random.key(seed)
    ks = jax.random.split(key, 24)
    f32 = jnp.float32
    nrm = lambda k, shape, sc: sc * jax.random.normal(k, shape, f32)
    gain = lambda k, shape: 1.0 + 0.02 * jax.random.normal(k, shape, f32)
    dt0 = jnp.exp(jax.random.uniform(ks[9], (DEPTH, SSD_HEADS), f32, math.log(1e-3), math.log(1e-1)))
    return {
        "x": nrm(ks[0], (BATCH, SEQ, D_MODEL), 1.0),
        "mix_norm_g": gain(ks[1], (DEPTH, D_MODEL)),
        "w_in": nrm(ks[2], (DEPTH, D_MODEL, N_IN), D_MODEL ** -0.5),
        "fox_f_bias": jax.random.uniform(ks[3], (DEPTH, FOX_HEADS), f32, 1.0, 6.0),
        "fox_out_g": gain(ks[4], (DEPTH, D_FOX)),
        "sb_out_g": gain(ks[5], (DEPTH, D_SB)),
        "ssd_conv_w": nrm(ks[6], (DEPTH, SSD_CONV, SSD_CONV_CH), SSD_CONV ** -0.5),
        "ssd_conv_b": nrm(ks[7], (DEPTH, SSD_CONV_CH), 0.02),
        "ssd_dt_bias": dt0 + jnp.log(-jnp.expm1(-dt0)),
        "ssd_a_log": jnp.log(jax.random.uniform(ks[10], (DEPTH, SSD_HEADS), f32, 1.0, 16.0)),
        "ssd_d": gain(ks[11], (DEPTH, SSD_HEADS)),
        "ssd_norm_g": gain(ks[12], (DEPTH, D_SSD)),
        "w_out": nrm(ks[13], (DEPTH, D_MIX, D_MODEL), D_MIX ** -0.5),
        "ffn_norm_g": gain(ks[14], (DEPTH, D_MODEL)),
        "w_up": nrm(ks[15], (DEPTH, D_MODEL, 2 * D_FF), D_MODEL ** -0.5),
        "ffn_conv_w": nrm(ks[16], (DEPTH, FFN_CONV, 2 * D_FF), FFN_CONV ** -0.5),
        "ffn_conv_b": nrm(ks[17], (DEPTH, 2 * D_FF), 0.02),
        "w_down": nrm(ks[18], (DEPTH, D_FF, D_MODEL), D_FF ** -0.5),
        "final_norm_g": gain(ks[19], (D_MODEL,)),
    }


def reference(x, mix_norm_g, w_in, fox_f_bias, fox_out_g, sb_out_g, ssd_conv_w, ssd_conv_b,
              ssd_dt_bias, ssd_a_log, ssd_d, ssd_norm_g, w_out, ffn_norm_g, w_up,
              ffn_conv_w, ffn_conv_b, w_down, final_norm_g):
    for l in range(DEPTH):
        x = hybrid_layer(x, mix_norm_g[l], w_in[l], fox_f_bias[l], fox_out_g[l], sb_out_g[l],
                         ssd_conv_w[l], ssd_conv_b[l], ssd_dt_bias[l], ssd_a_log[l], ssd_d[l],
                         ssd_norm_g[l], w_out[l], ffn_norm_g[l], w_up[l], ffn_conv_w[l],
                         ffn_conv_b[l], w_down[l])
    return rms_norm(x, final_norm_g)
```

```python
import functools

import jax
import jax.numpy as jnp
from jax import lax
from jax.experimental import pallas as pl
from jax.experimental.pallas import tpu as pltpu

F32 = jnp.float32
BF16 = jnp.bfloat16

D_MODEL = 1024
HEAD_DIM = 64
FOX_HEADS = 8
SB_HEADS = 8
SSD_HEADS = 16
SSD_GROUPS = 2
SSD_STATE = 128
SSD_CONV = 4
SSD_CHUNK = 128
D_FOX = FOX_HEADS * HEAD_DIM
D_SB = SB_HEADS * HEAD_DIM
D_SSD = SSD_HEADS * HEAD_DIM
D_BC = SSD_GROUPS * SSD_STATE
D_FF = 2816
FFN_CONV = 3
NORM_EPS = 1e-6

LANES = 128
N_MAIN = 3 * D_FOX + 3 * D_SB + D_SSD + D_SSD + 2 * D_BC
OFF_FOX = 0
OFF_SB = 3 * D_FOX
OFF_Z = OFF_SB + 3 * D_SB
OFF_XS = OFF_Z + D_SSD
OFF_BC = OFF_XS + D_SSD
SMALL_DT = 0
SMALL_FF = SSD_HEADS

VMEM_LIMIT = 56 * 1024 * 1024

TM_PROJ = 512
TQ = 256
FFN_FC = 256


def _cparams(sem):
    return pltpu.CompilerParams(dimension_semantics=sem, vmem_limit_bytes=VMEM_LIMIT)


def _split_bf16(x, n):
    parts, r = [], x
    for _ in range(n):
        p = r.astype(BF16)
        parts.append(p)
        r = r - p.astype(F32)
    return parts


def _dot(a, b):
    return jnp.dot(a, b, preferred_element_type=F32)


def _dot_nt(a, b):
    return lax.dot_general(a, b, (((1,), (1,)), ((), ())), preferred_element_type=F32)


def _dot_tn(a, b):
    return lax.dot_general(a, b, (((0,), (0,)), ((), ())), preferred_element_type=F32)


def _mask_dot_left(m01, x, n):
    out = None
    for p in _split_bf16(x, n):
        t = _dot(m01, p)
        out = t if out is None else out + t
    return out


def _mask_dot_right(x, m01, n):
    out = None
    for p in _split_bf16(x, n):
        t = _dot(p, m01)
        out = t if out is None else out + t
    return out


def _softplus_neg_abs(x):
    return jnp.log1p(jnp.exp(-jnp.abs(x)))


def _sigmoid(x):
    return 1.0 / (1.0 + jnp.exp(-x))


def _rms(x, g):
    ms = jnp.mean(x * x, axis=-1, keepdims=True)
    return x * lax.rsqrt(ms + NORM_EPS) * g


def _head_pair_norm(o2, g, lane):
    lo = lane < HEAD_DIM
    sq = o2 * o2
    ms0 = jnp.sum(jnp.where(lo, sq, 0.0), axis=-1, keepdims=True) * (1.0 / HEAD_DIM)
    ms1 = jnp.sum(jnp.where(lo, 0.0, sq), axis=-1, keepdims=True) * (1.0 / HEAD_DIM)
    ms = jnp.where(lo, ms0, ms1)
    return o2 * lax.rsqrt(ms + NORM_EPS) * g


def _inproj_kernel(x_ref, g_ref, wm_ref, ws_ref, main_ref, small_ref, *, nc):
    h = _rms(x_ref[...], g_ref[...]).astype(BF16)
    for c in range(0, N_MAIN, nc):
        main_ref[:, c:c + nc] = _dot(h, wm_ref[:, c:c + nc]).astype(BF16)
    small_ref[...] = _dot(h, ws_ref[...])


def _inproj(x2, g, wm, ws):
    t = x2.shape[0]
    tm = TM_PROJ
    return pl.pallas_call(
        functools.partial(_inproj_kernel, nc=512),
        grid=(t // tm,),
        in_specs=[
            pl.BlockSpec((tm, D_MODEL), lambda i: (i, 0)),
            pl.BlockSpec((1, D_MODEL), lambda i: (0, 0)),
            pl.BlockSpec((D_MODEL, N_MAIN), lambda i: (0, 0)),
            pl.BlockSpec((D_MODEL, LANES), lambda i: (0, 0)),
        ],
        out_specs=[
            pl.BlockSpec((tm, N_MAIN), lambda i: (i, 0)),
            pl.BlockSpec((tm, LANES), lambda i: (i, 0)),
        ],
        out_shape=[
            jax.ShapeDtypeStruct((t, N_MAIN), BF16),
            jax.ShapeDtypeStruct((t, LANES), F32),
        ],
        compiler_params=_cparams(("arbitrary",)),
        name="inproj",
    )(x2, g, wm, ws)


def _fox_gate_kernel(s_ref, bias_ref, c_ref, carry_ref, *, tb):
    @pl.when(pl.program_id(1) == 0)
    def _():
        carry_ref[...] = jnp.zeros_like(carry_ref)

    xx = s_ref[0] + bias_ref[...]
    log_f = jnp.minimum(xx, 0.0) - _softplus_neg_abs(xx)
    row = lax.broadcasted_iota(jnp.int32, (tb, tb), 0)
    col = lax.broadcasted_iota(jnp.int32, (tb, tb), 1)
    tri = jnp.where(row >= col, 1.0, 0.0).astype(BF16)
    cum = _mask_dot_left(tri, log_f, 3) + carry_ref[...]
    c_ref[0] = cum
    carry_ref[...] = cum[tb - 1:tb, :]


def _fox_gate(small3, bias_row):
    b, s, _ = small3.shape
    tb = 512
    return pl.pallas_call(
        functools.partial(_fox_gate_kernel, tb=tb),
        grid=(b, s // tb),
        in_specs=[
            pl.BlockSpec((1, tb, LANES), lambda bi, i: (bi, i, 0)),
            pl.BlockSpec((1, LANES), lambda bi, i: (0, 0)),
        ],
        out_specs=pl.BlockSpec((1, tb, LANES), lambda bi, i: (bi, i, 0)),
        out_shape=jax.ShapeDtypeStruct((b, s, LANES), F32),
        scratch_shapes=[pltpu.VMEM((1, LANES), F32)],
        compiler_params=_cparams(("arbitrary", "arbitrary")),
        name="fox_gate",
    )(small3, bias_row)


def _fox_kernel(q_ref, k_ref, v_ref, cq_ref, ck_ref, g_ref, o_ref, m_ref, l_ref, acc_ref, *, tq):
    i = pl.program_id(2)
    lane = lax.broadcasted_iota(jnp.int32, (1, LANES), 1)
    q2 = q_ref[0]
    zero = jnp.zeros_like(q2)
    qh = [jnp.where(lane < HEAD_DIM, q2, zero) * 0.125,
          jnp.where(lane < HEAD_DIM, zero, q2) * 0.125]
    cq = [cq_ref[0, 0, :, 0:1], cq_ref[0, 0, :, 1:2]]

    m_ref[...] = jnp.full_like(m_ref, -jnp.inf)
    l_ref[...] = jnp.zeros_like(l_ref)
    acc_ref[...] = jnp.zeros_like(acc_ref)

    def tile(j, masked):
        start = pl.multiple_of(j * tq, tq)
        k_t = k_ref[0, pl.ds(start, tq), :]
        v_t = v_ref[0, pl.ds(start, tq), :]
        for hh in range(2):
            s = _dot_nt(qh[hh], k_t)
            s = s + (cq[hh] - ck_ref[0, 0, hh:hh + 1, pl.ds(start, tq)])
            if masked:
                row = lax.broadcasted_iota(jnp.int32, (tq, tq), 0)
                col = lax.broadcasted_iota(jnp.int32, (tq, tq), 1)
                s = jnp.where(row >= col, s, -jnp.inf)
            m_prev = m_ref[hh]
            m_new = jnp.maximum(m_prev, jnp.max(s, axis=-1, keepdims=True))
            alpha = jnp.exp(m_prev - m_new)
            p = jnp.exp(s - m_new)
            l_ref[hh] = alpha * l_ref[hh] + jnp.sum(p, axis=-1, keepdims=True)
            acc_ref[hh] = alpha * acc_ref[hh] + _dot(p.astype(BF16), v_t)
            m_ref[hh] = m_new

    def body(j, carry):
        tile(j, False)
        return carry

    lax.fori_loop(0, i, body, 0)
    tile(i, True)

    o0 = acc_ref[0] / l_ref[0]
    o1 = acc_ref[1] / l_ref[1]
    o2 = jnp.where(lane < HEAD_DIM, o0, o1)
    o_ref[0] = _head_pair_norm(o2, g_ref[...], lane).astype(BF16)


def _fox_attention(main3, cq, ck, gain):
    b, s, _ = main3.shape
    tq = TQ
    npair = FOX_HEADS // 2
    qb, kb, vb = OFF_FOX // LANES, (OFF_FOX + D_FOX) // LANES, (OFF_FOX + 2 * D_FOX) // LANES
    return pl.pallas_call(
        functools.partial(_fox_kernel, tq=tq),
        grid=(b, npair, s // tq),
        in_specs=[
            pl.BlockSpec((1, tq, LANES), lambda bi, p, i: (bi, i, qb + p)),
            pl.BlockSpec((1, s, LANES), lambda bi, p, i: (bi, 0, kb + p)),
            pl.BlockSpec((1, s, LANES), lambda bi, p, i: (bi, 0, vb + p)),
            pl.BlockSpec((1, 1, tq, 2), lambda bi, p, i: (bi, p, i, 0)),
            pl.BlockSpec((1, 1, 2, s), lambda bi, p, i: (bi, p, 0, 0)),
            pl.BlockSpec((1, LANES), lambda bi, p, i: (0, p)),
        ],
        out_specs=pl.BlockSpec((1, tq, LANES), lambda bi, p, i: (bi, i, p)),
        out_shape=jax.ShapeDtypeStruct((b, s, D_FOX), BF16),
        scratch_shapes=[
            pltpu.VMEM((2, tq, 1), F32),
            pltpu.VMEM((2, tq, 1), F32),
            pltpu.VMEM((2, tq, LANES), F32),
        ],
        compiler_params=_cparams(("arbitrary", "arbitrary", "arbitrary")),
        name="fox_attn",
    )(main3, main3, main3, cq, ck, gain)


def _sb_kernel(q_ref, k_ref, v_ref, g_ref, o_ref, r_ref, acc_ref, *, tq):
    i = pl.program_id(2)
    lane = lax.broadcasted_iota(jnp.int32, (1, LANES), 1)
    q2 = q_ref[0]
    zero = jnp.zeros_like(q2)
    qh = [jnp.where(lane < HEAD_DIM, q2, zero) * 0.125,
          jnp.where(lane < HEAD_DIM, zero, q2) * 0.125]
    row = lax.broadcasted_iota(jnp.int32, (tq, tq), 0)
    col = lax.broadcasted_iota(jnp.int32, (tq, tq), 1)
    later = jnp.where(row > col, 1.0, 0.0).astype(BF16)

    r_ref[...] = jnp.zeros_like(r_ref)
    acc_ref[...] = jnp.zeros_like(acc_ref)

    def tile(j, masked):
        start = pl.multiple_of(j * tq, tq)
        k_t = k_ref[0, pl.ds(start, tq), :]
        v_t = v_ref[0, pl.ds(start, tq), :]
        for hh in range(2):
            z = _dot_nt(qh[hh], k_t)
            sp = _softplus_neg_abs(z)
            log_beta = jnp.minimum(z, 0.0) - sp
            log_keep = -jnp.maximum(z, 0.0) - sp
            if masked:
                log_keep = jnp.where(col < row, log_keep, 0.0)
            suffix = _mask_dot_right(log_keep, later, 2)
            log_w = log_beta + r_ref[hh] + suffix
            w = jnp.exp(log_w)
            if masked:
                w = jnp.where(col < row, w, 0.0)
            acc_ref[hh] = acc_ref[hh] + _dot(w.astype(BF16), v_t)
            r_ref[hh] = r_ref[hh] + jnp.sum(log_keep, axis=-1, keepdims=True)

    tile(i, True)

    def body(it, carry):
        tile(i - 1 - it, False)
        return carry

    lax.fori_loop(0, i, body, 0)

    o2 = jnp.where(lane < HEAD_DIM, acc_ref[0], acc_ref[1])
    o_ref[0] = _head_pair_norm(o2, g_ref[...], lane).astype(BF16)


def _sb_attention(main3, gain):
    b, s, _ = main3.shape
    tq = TQ
    npair = SB_HEADS // 2
    qb, kb, vb = OFF_SB // LANES, (OFF_SB + D_SB) // LANES, (OFF_SB + 2 * D_SB) // LANES
    return pl.pallas_call(
        functools.partial(_sb_kernel, tq=tq),
        grid=(b, npair, s // tq),
        in_specs=[
            pl.BlockSpec((1, tq, LANES), lambda bi, p, i: (bi, i, qb + p)),
            pl.BlockSpec((1, s, LANES), lambda bi, p, i: (bi, 0, kb + p)),
            pl.BlockSpec((1, s, LANES), lambda bi, p, i: (bi, 0, vb + p)),
            pl.BlockSpec((1, LANES), lambda bi, p, i: (0, p)),
        ],
        out_specs=pl.BlockSpec((1, tq, LANES), lambda bi, p, i: (bi, i, p)),
        out_shape=jax.ShapeDtypeStruct((b, s, D_SB), BF16),
        scratch_shapes=[
            pltpu.VMEM((2, tq, 1), F32),
            pltpu.VMEM((2, tq, LANES), F32),
        ],
        compiler_params=_cparams(("arbitrary", "arbitrary", "arbitrary")),
        name="sb_attn",
    )(main3, main3, main3, gain)


def _ssd_kernel(z_ref, xs_ref, bc_ref, dt_ref, cwx_ref, cbx_ref, cwb_ref, cbb_ref, dtb_ref,
                alog_ref, dexp_ref, ng_ref, e_ref, o_ref,
                extx_ref, extb_ref, state_ref):
    L = SSD_CHUNK
    hpg = SSD_HEADS // SSD_GROUPS
    gw = hpg * HEAD_DIM

    @pl.when(pl.program_id(1) == 0)
    def _():
        extx_ref[0:8, :] = jnp.zeros((8, D_SSD), F32)
        extb_ref[0:8, :] = jnp.zeros((8, 2 * D_BC), F32)
        state_ref[...] = jnp.zeros_like(state_ref)

    def conv_silu(raw_ref, ext_ref, w_ref, b_ref):
        raw = raw_ref[0].astype(F32)
        ext_ref[8:8 + L, :] = raw
        y = (b_ref[...] + w_ref[3:4, :] * raw
             + w_ref[2:3, :] * ext_ref[7:7 + L, :]
             + w_ref[1:2, :] * ext_ref[6:6 + L, :]
             + w_ref[0:1, :] * ext_ref[5:5 + L, :])
        ext_ref[0:8, :] = raw[L - 8:L, :]
        return y * _sigmoid(y)

    xs = conv_silu(xs_ref, extx_ref, cwx_ref, cbx_ref)
    bc = conv_silu(bc_ref, extb_ref, cwb_ref, cbb_ref)

    xdt_raw = dt_ref[0] + dtb_ref[...]
    dt = jnp.maximum(xdt_raw, 0.0) + _softplus_neg_abs(xdt_raw)
    a = -jnp.exp(alog_ref[...])
    da = dt * a
    row = lax.broadcasted_iota(jnp.int32, (L, L), 0)
    col = lax.broadcasted_iota(jnp.int32, (L, L), 1)
    causal = row >= col
    tri = jnp.where(causal, 1.0, 0.0).astype(BF16)
    a_cs = _mask_dot_left(tri, da, 3)
    a_cs_t = a_cs.T
    a_last = a_cs[L - 1:L, :]

    e01 = e_ref[...]
    dt_x = _mask_dot_right(dt, e01, 3)
    dec_in_x = _mask_dot_right(jnp.exp(a_cs), e01, 3)
    dec_end_x = _mask_dot_right(jnp.exp(a_last - a_cs), e01, 3)
    chunk_dec_x = _mask_dot_right(jnp.exp(a_last), e01, 3)

    xdt = xs * dt_x
    xdt_b = xdt.astype(BF16)
    xend_b = (xdt * dec_end_x).astype(BF16)

    lane = lax.broadcasted_iota(jnp.int32, (1, LANES), 1)
    ys = []
    for g in range(SSD_GROUPS):
        b_g = bc[:, g * SSD_STATE:(g + 1) * SSD_STATE].astype(BF16)
        c_g = bc[:, D_BC + g * SSD_STATE:D_BC + (g + 1) * SSD_STATE].astype(BF16)
        cb = _dot_nt(c_g, b_g)
        st = state_ref[g]
        y_off = _dot(c_g, st.astype(BF16)) * dec_in_x[:, g * gw:(g + 1) * gw]
        y_diag = []
        for pr in range(hpg // 2):
            outs = []
            for hh in range(2):
                h = g * hpg + 2 * pr + hh
                seg = jnp.exp(jnp.where(causal, a_cs[:, h:h + 1] - a_cs_t[h:h + 1, :], -jnp.inf))
                m = (cb * seg).astype(BF16)
                lo = (g * hpg + 2 * pr) * HEAD_DIM
                outs.append(_dot(m, xdt_b[:, lo:lo + LANES]))
            y_diag.append(jnp.where(lane < HEAD_DIM, outs[0], outs[1]))
        y_g = jnp.concatenate(y_diag, axis=-1) + y_off
        ys.append(y_g)
        state_ref[g] = (chunk_dec_x[:, g * gw:(g + 1) * gw] * st
                        + _dot_tn(b_g, xend_b[:, g * gw:(g + 1) * gw]))

    zf = z_ref[0].astype(F32)
    gate = zf * _sigmoid(zf)
    for g in range(SSD_GROUPS):
        sl = slice(g * gw, (g + 1) * gw)
        y_g = (ys[g] + xs[:, sl] * dexp_ref[:, sl]) * gate[:, sl]
        o_ref[0, :, sl] = _rms(y_g, ng_ref[:, sl]).astype(BF16)


def _ssd(main3, small3, cw, cb, dtb, alog, dexp, ng, e01):
    b, s, _ = main3.shape
    L = SSD_CHUNK
    zb, xb, bcb = OFF_Z // D_SSD, OFF_XS // D_SSD, OFF_BC // (2 * D_BC)
    cwx, cwb = cw[:, :D_SSD], cw[:, D_SSD:]
    cbx, cbb = cb[:, :D_SSD], cb[:, D_SSD:]
    const = lambda shape: pl.BlockSpec(shape, lambda bi, c: (0,) * len(shape))
    return pl.pallas_call(
        _ssd_kernel,
        grid=(b, s // L),
        in_specs=[
            pl.BlockSpec((1, L, D_SSD), lambda bi, c: (bi, c, zb)),
            pl.BlockSpec((1, L, D_SSD), lambda bi, c: (bi, c, xb)),
            pl.BlockSpec((1, L, 2 * D_BC), lambda bi, c: (bi, c, bcb)),
            pl.BlockSpec((1, L, LANES), lambda bi, c: (bi, c, 0)),
            const((SSD_CONV, D_SSD)), const((1, D_SSD)),
            const((SSD_CONV, 2 * D_BC)), const((1, 2 * D_BC)),
            const((1, LANES)), const((1, LANES)),
            const((1, D_SSD)), const((1, D_SSD)),
            const((LANES, D_SSD)),
        ],
        out_specs=pl.BlockSpec((1, L, D_SSD), lambda bi, c: (bi, c, 0)),
        out_shape=jax.ShapeDtypeStruct((b, s, D_SSD), BF16),
        scratch_shapes=[
            pltpu.VMEM((L + 8, D_SSD), F32),
            pltpu.VMEM((L + 8, 2 * D_BC), F32),
            pltpu.VMEM((SSD_GROUPS, SSD_STATE, D_SSD // SSD_GROUPS), F32),
        ],
        compiler_params=_cparams(("arbitrary", "arbitrary")),
        name="ssd",
    )(main3, main3, main3, small3, cwx, cbx, cwb, cbb, dtb, alog, dexp, ng, e01)


def _outproj_kernel(x_ref, yf_ref, ys_ref, yd_ref, wf_ref, ws_ref, wd_ref, o_ref):
    o_ref[...] = (x_ref[...] + _dot(yf_ref[...], wf_ref[...]) + _dot(ys_ref[...], ws_ref[...])
                  + _dot(yd_ref[...], wd_ref[...]))


def _outproj(x2, yf, ys, yd, wf, ws, wd):
    t = x2.shape[0]
    tm = TM_PROJ
    return pl.pallas_call(
        _outproj_kernel,
        grid=(t // tm,),
        in_specs=[
            pl.BlockSpec((tm, D_MODEL), lambda i: (i, 0)),
            pl.BlockSpec((tm, D_FOX), lambda i: (i, 0)),
            pl.BlockSpec((tm, D_SB), lambda i: (i, 0)),
            pl.BlockSpec((tm, D_SSD), lambda i: (i, 0)),
            pl.BlockSpec((D_FOX, D_MODEL), lambda i: (0, 0)),
            pl.BlockSpec((D_SB, D_MODEL), lambda i: (0, 0)),
            pl.BlockSpec((D_SSD, D_MODEL), lambda i: (0, 0)),
        ],
        out_specs=pl.BlockSpec((tm, D_MODEL), lambda i: (i, 0)),
        out_shape=jax.ShapeDtypeStruct((t, D_MODEL), F32),
        compiler_params=_cparams(("arbitrary",)),
        name="outproj",
    )(x2, yf, ys, yd, wf, ws, wd)


def _ffn_kernel(x_ref, g_ref, wu_ref, cw_ref, cb_ref, wd_ref, fg_ref, o_ref,
                carry_ref, ext_ref, acc_ref, *, tm, tiles_per_seq, final_norm):
    fc = FFN_FC
    cw2 = 2 * fc

    @pl.when(pl.program_id(0) % tiles_per_seq == 0)
    def _():
        carry_ref[...] = jnp.zeros_like(carry_ref)

    x = x_ref[...]
    h = _rms(x, g_ref[...]).astype(BF16)
    acc_ref[...] = x
    for c in range(D_FF // fc):
        cs = slice(c * cw2, (c + 1) * cw2)
        u = _dot(h, wu_ref[:, cs])
        ext_ref[0:8, :] = carry_ref[:, cs]
        ext_ref[8:8 + tm, :] = u
        carry_ref[:, cs] = u[tm - 8:tm, :]
        y = (cb_ref[:, cs] + cw_ref[2:3, cs] * u
             + cw_ref[1:2, cs] * ext_ref[7:7 + tm, :]
             + cw_ref[0:1, cs] * ext_ref[6:6 + tm, :])
        gate = y[:, :fc]
        act = (gate * _sigmoid(gate) * y[:, fc:]).astype(BF16)
        acc_ref[...] += _dot(act, wd_ref[c * fc:(c + 1) * fc, :])
    out = acc_ref[...]
    if final_norm:
        out = _rms(out, fg_ref[...])
    o_ref[...] = out


def _ffn(x2, g, wu, cw, cb, wd, fg, seq, final_norm):
    t = x2.shape[0]
    tm = TM_PROJ
    return pl.pallas_call(
        functools.partial(_ffn_kernel, tm=tm, tiles_per_seq=seq // tm, final_norm=final_norm),
        grid=(t // tm,),
        in_specs=[
            pl.BlockSpec((tm, D_MODEL), lambda i: (i, 0)),
            pl.BlockSpec((1, D_MODEL), lambda i: (0, 0)),
            pl.BlockSpec((D_MODEL, 2 * D_FF), lambda i: (0, 0)),
            pl.BlockSpec((FFN_CONV, 2 * D_FF), lambda i: (0, 0)),
            pl.BlockSpec((1, 2 * D_FF), lambda i: (0, 0)),
            pl.BlockSpec((D_FF, D_MODEL), lambda i: (0, 0)),
            pl.BlockSpec((1, D_MODEL), lambda i: (0, 0)),
        ],
        out_specs=pl.BlockSpec((tm, D_MODEL), lambda i: (i, 0)),
        out_shape=jax.ShapeDtypeStruct((t, D_MODEL), F32),
        scratch_shapes=[
            pltpu.VMEM((8, 2 * D_FF), F32),
            pltpu.VMEM((tm + 8, 2 * FFN_FC), F32),
            pltpu.VMEM((tm, D_MODEL), F32),
        ],
        compiler_params=_cparams(("arbitrary",)),
        name="ffn",
    )(x2, g, wu, cw, cb, wd, fg)


def _interleave_ff(a):
    lead = a.shape[:-1]
    n = D_FF // FFN_FC
    a = a.reshape(lead + (2, n, FFN_FC))
    a = jnp.swapaxes(a, -3, -2)
    return a.reshape(lead + (2 * D_FF,))


def _pad_lanes(v, offset):
    return jnp.zeros((1, LANES), F32).at[0, offset:offset + v.shape[0]].set(v.astype(F32))


def _layer(x2, b, s, mix_g, w_in, fox_f_bias, fox_out_g, sb_out_g, ssd_conv_w, ssd_conv_b,
           ssd_dt_bias, ssd_a_log, ssd_d, ssd_norm_g, w_out, ffn_g, w_up, ffn_conv_w, ffn_conv_b,
           w_down, final_g, final_norm):
    o = 0
    cols = {}
    for name, n in (("fq", D_FOX), ("fk", D_FOX), ("fv", D_FOX), ("ff", FOX_HEADS),
                    ("sq", D_SB), ("sk", D_SB), ("sv", D_SB),
                    ("z", D_SSD), ("xbc", D_SSD + 2 * D_BC), ("dt", SSD_HEADS)):
        cols[name] = w_in[:, o:o + n]
        o += n
    wm = jnp.concatenate([cols[k] for k in ("fq", "fk", "fv", "sq", "sk", "sv", "z", "xbc")],
                         axis=1).astype(BF16)
    ws = jnp.concatenate([cols["dt"], cols["ff"],
                          jnp.zeros((D_MODEL, LANES - SSD_HEADS - FOX_HEADS), F32)], axis=1).astype(BF16)

    main2, small2 = _inproj(x2, mix_g[None, :], wm, ws)
    main3 = main2.reshape(b, s, N_MAIN)
    small3 = small2.reshape(b, s, LANES)

    c_full = _fox_gate(small3, _pad_lanes(fox_f_bias, SMALL_FF))
    c8 = c_full[:, :, SMALL_FF:SMALL_FF + FOX_HEADS]
    cq = c8.reshape(b, s, FOX_HEADS // 2, 2).transpose(0, 2, 1, 3)
    ck = c8.transpose(0, 2, 1).reshape(b, FOX_HEADS // 2, 2, s)
    y_fox = _fox_attention(main3, cq, ck, fox_out_g[None, :])
    y_sb = _sb_attention(main3, sb_out_g[None, :])

    e01 = (jnp.arange(LANES)[:, None] == (jnp.arange(D_SSD)[None, :] // HEAD_DIM)).astype(BF16)
    y_ssd = _ssd(main3, small3, ssd_conv_w, ssd_conv_b[None, :],
                 _pad_lanes(ssd_dt_bias, SMALL_DT), _pad_lanes(ssd_a_log, SMALL_DT),
                 jnp.repeat(ssd_d, HEAD_DIM)[None, :], ssd_norm_g[None, :], e01)

    t = b * s
    wo = w_out.astype(BF16)
    x2 = _outproj(x2, y_fox.reshape(t, D_FOX), y_sb.reshape(t, D_SB), y_ssd.reshape(t, D_SSD),
                  wo[:D_FOX], wo[D_FOX:D_FOX + D_SB], wo[D_FOX + D_SB:])
    return _ffn(x2, ffn_g[None, :], _interleave_ff(w_up).astype(BF16), _interleave_ff(ffn_conv_w),
                _interleave_ff(ffn_conv_b)[None, :], w_down.astype(BF16), final_g[None, :], s,
                final_norm)


def kernel(x, mix_norm_g, w_in, fox_f_bias, fox_out_g, sb_out_g, ssd_conv_w, ssd_conv_b, ssd_dt_bias,
           ssd_a_log, ssd_d, ssd_norm_g, w_out, ffn_norm_g, w_up, ffn_conv_w, ffn_conv_b, w_down,
           final_norm_g):
    b, s, d = x.shape
    depth = w_in.shape[0]
    x2 = x.reshape(b * s, d)
    for l in range(depth):
        x2 = _layer(x2, b, s, mix_norm_g[l], w_in[l], fox_f_bias[l], fox_out_g[l], sb_out_g[l],
                    ssd_conv_w[l], ssd_conv_b[l], ssd_dt_bias[l], ssd_a_log[l], ssd_d[l],
                    ssd_norm_g[l], w_out[l], ffn_norm_g[l], w_up[l], ffn_conv_w[l], ffn_conv_b[l],
                    w_down[l], final_norm_g, l == depth - 1)
    return x2.reshape(b, s, d)
```

```python
import functools

import jax
import jax.numpy as jnp
from jax import lax
from jax.experimental import pallas as pl
from jax.experimental.pallas import tpu as pltpu

F32 = jnp.float32
BF16 = jnp.bfloat16

D_MODEL = 1024
HEAD_DIM = 64
FOX_HEADS = 8
SB_HEADS = 8
SSD_HEADS = 16
SSD_GROUPS = 2
SSD_STATE = 128
SSD_CONV = 4
SSD_CHUNK = 128
D_FOX = FOX_HEADS * HEAD_DIM
D_SB = SB_HEADS * HEAD_DIM
D_SSD = SSD_HEADS * HEAD_DIM
D_BC = SSD_GROUPS * SSD_STATE
D_FF = 2816
FFN_CONV = 3
NORM_EPS = 1e-6
CP_TERMS = 3
EXP_UNDERFLOW = -104.0

LANES = 128
N_MAIN = 3 * D_FOX + 3 * D_SB + D_SSD + D_SSD + 2 * D_BC
OFF_FOX = 0
OFF_SB = 3 * D_FOX
OFF_Z = OFF_SB + 3 * D_SB
OFF_XS = OFF_Z + D_SSD
OFF_BC = OFF_XS + D_SSD
SMALL_DT = 0
SMALL_FF = SSD_HEADS

VMEM_LIMIT = 56 * 1024 * 1024

TM_PROJ = 512
TQ = 256
TQ_FOX = 512
SB_PAIRS_PER_STEP = 2
FFN_FC = 256


def _cparams(sem):
    return pltpu.CompilerParams(dimension_semantics=sem, vmem_limit_bytes=VMEM_LIMIT)


def _split_bf16(x, n):
    parts, r = [], x
    for _ in range(n):
        p = r.astype(BF16)
        parts.append(p)
        r = r - p.astype(F32)
    return parts


def _dot(a, b):
    return jnp.dot(a, b, preferred_element_type=F32)


def _dot_nt(a, b):
    return lax.dot_general(a, b, (((1,), (1,)), ((), ())), preferred_element_type=F32)


def _dot_tn(a, b):
    return lax.dot_general(a, b, (((0,), (0,)), ((), ())), preferred_element_type=F32)


def _mask_dot_left(m01, x, n):
    out = None
    for p in _split_bf16(x, n):
        t = _dot(m01, p)
        out = t if out is None else out + t
    return out


def _mask_dot_right(x, m01, n):
    out = None
    for p in _split_bf16(x, n):
        t = _dot(p, m01)
        out = t if out is None else out + t
    return out


def _softplus_neg_abs(x):
    return jnp.log(1.0 + jnp.exp(-jnp.abs(x)))


def _sigmoid(x):
    return 1.0 / (1.0 + jnp.exp(-x))


def _rms(x, g):
    ms = jnp.mean(x * x, axis=-1, keepdims=True)
    return x * lax.rsqrt(ms + NORM_EPS) * g


def _head_pair_norm(o2, g, lane):
    lo = lane < HEAD_DIM
    sq = o2 * o2
    ms0 = jnp.sum(jnp.where(lo, sq, 0.0), axis=-1, keepdims=True) * (1.0 / HEAD_DIM)
    ms1 = jnp.sum(jnp.where(lo, 0.0, sq), axis=-1, keepdims=True) * (1.0 / HEAD_DIM)
    ms = jnp.where(lo, ms0, ms1)
    return o2 * lax.rsqrt(ms + NORM_EPS) * g


def _inproj_kernel(x_ref, g_ref, wm_ref, ws_ref, main_ref, small_ref, *, nc):
    h = _rms(x_ref[...], g_ref[...]).astype(BF16)
    for c in range(0, N_MAIN, nc):
        main_ref[:, c:c + nc] = _dot(h, wm_ref[:, c:c + nc]).astype(BF16)
    small_ref[...] = _dot(h, ws_ref[...])


def _inproj(x2, g, wm, ws):
    t = x2.shape[0]
    tm = TM_PROJ
    return pl.pallas_call(
        functools.partial(_inproj_kernel, nc=512),
        grid=(t // tm,),
        in_specs=[
            pl.BlockSpec((tm, D_MODEL), lambda i: (i, 0)),
            pl.BlockSpec((1, D_MODEL), lambda i: (0, 0)),
            pl.BlockSpec((D_MODEL, N_MAIN), lambda i: (0, 0)),
            pl.BlockSpec((D_MODEL, LANES), lambda i: (0, 0)),
        ],
        out_specs=[
            pl.BlockSpec((tm, N_MAIN), lambda i: (i, 0)),
            pl.BlockSpec((tm, LANES), lambda i: (i, 0)),
        ],
        out_shape=[
            jax.ShapeDtypeStruct((t, N_MAIN), BF16),
            jax.ShapeDtypeStruct((t, LANES), F32),
        ],
        compiler_params=_cparams(("arbitrary",)),
        name="inproj",
    )(x2, g, wm, ws)


def _fox_gate_kernel(s_ref, bias_ref, c_ref, cp_ref, carry_ref, *, tb):
    @pl.when(pl.program_id(1) == 0)
    def _():
        carry_ref[...] = jnp.zeros_like(carry_ref)

    xx = s_ref[0] + bias_ref[...]
    log_f = jnp.minimum(xx, 0.0) - _softplus_neg_abs(xx)
    row = lax.broadcasted_iota(jnp.int32, (tb, tb), 0)
    col = lax.broadcasted_iota(jnp.int32, (tb, tb), 1)
    tri = jnp.where(row >= col, 1.0, 0.0).astype(BF16)
    cum = _mask_dot_left(tri, log_f, 3) + carry_ref[...]
    c_ref[0] = cum
    carry_ref[...] = cum[tb - 1:tb, :]
    src = lax.broadcasted_iota(jnp.int32, (LANES, LANES), 0) - SMALL_FF
    dst = lax.broadcasted_iota(jnp.int32, (LANES, LANES), 1)
    head_ok = (src >= 0) & (src < FOX_HEADS)
    cp = None
    for j, part in enumerate(_split_bf16(-cum, CP_TERMS)):
        sel = jnp.where(head_ok & (dst == CP_TERMS * src + j), 1.0, 0.0).astype(BF16)
        t = _dot(part, sel)
        cp = t if cp is None else cp + t
    cp_ref[0] = cp.astype(BF16)


def _fox_gate(small3, bias_row):
    b, s, _ = small3.shape
    tb = 512
    blk = pl.BlockSpec((1, tb, LANES), lambda bi, i: (bi, i, 0))
    return pl.pallas_call(
        functools.partial(_fox_gate_kernel, tb=tb),
        grid=(b, s // tb),
        in_specs=[blk, pl.BlockSpec((1, LANES), lambda bi, i: (0, 0))],
        out_specs=[blk, blk],
        out_shape=[jax.ShapeDtypeStruct((b, s, LANES), F32),
                   jax.ShapeDtypeStruct((b, s, LANES), BF16)],
        scratch_shapes=[pltpu.VMEM((1, LANES), F32)],
        compiler_params=_cparams(("arbitrary", "arbitrary")),
        name="fox_gate",
    )(small3, bias_row)


def _fox_kernel(q_ref, k_ref, v_ref, cp_ref, cq_ref, g_ref, o_ref, vh_ref, m_ref, acc_ref, *, tq):
    p_idx = pl.program_id(1)
    i = pl.program_id(2)
    lane = lax.broadcasted_iota(jnp.int32, (1, LANES), 1)
    lo_half = lane < HEAD_DIM

    @pl.when(i == 0)
    def _():
        v2 = v_ref[0]
        one = jnp.ones_like(v2)
        vh_ref[0] = jnp.where(lo_half, v2, one)
        vh_ref[1] = jnp.where(lo_half, one, v2)

    q2 = q_ref[0]
    zero = jnp.zeros_like(q2)
    q_aug, c_t = [], []
    for hh in range(2):
        own = lo_half if hh == 0 else jnp.logical_not(lo_half)
        first = CP_TERMS * (2 * p_idx + hh)
        ones_at = jnp.where((lane >= first) & (lane < first + CP_TERMS), 1.0, 0.0).astype(BF16)
        q_aug.append(jnp.concatenate(
            [jnp.where(own, q2, zero) * 0.125, jnp.broadcast_to(ones_at, (tq, LANES))], axis=1))
        c_t.append(jnp.broadcast_to(cq_ref[0, 0, :, hh:hh + 1], (tq, LANES)))

    m_ref[...] = jnp.full_like(m_ref, -jnp.inf)
    acc_ref[...] = jnp.zeros_like(acc_ref)

    def tile(j, masked):
        start = pl.multiple_of(j * tq, tq)
        k_aug = jnp.concatenate([k_ref[0, pl.ds(start, tq), :], cp_ref[0, pl.ds(start, tq), :]], axis=1)
        for hh in range(2):
            t = _dot_nt(q_aug[hh], k_aug)
            if masked:
                row = lax.broadcasted_iota(jnp.int32, (tq, tq), 0)
                col = lax.broadcasted_iota(jnp.int32, (tq, tq), 1)
                t = jnp.where(row >= col, t, -jnp.inf)
            m_prev = m_ref[hh]
            m_new = jnp.maximum(m_prev, jnp.max(t, axis=-1, keepdims=True) + c_t[hh])
            shift = m_new - c_t[hh]
            p = jnp.concatenate(
                [jnp.exp(t[:, c:c + LANES] - shift).astype(BF16) for c in range(0, tq, LANES)], axis=1)
            acc_ref[hh] = (jnp.exp(m_prev - m_new) * acc_ref[hh]
                           + _dot(p, vh_ref[hh, pl.ds(start, tq), :]))
            m_ref[hh] = m_new

    def body(j, carry):
        tile(j, False)
        return carry

    lax.fori_loop(0, i, body, 0)
    tile(i, True)

    a0, a1 = acc_ref[0], acc_ref[1]
    num = jnp.where(lo_half, a0, a1)
    den = pltpu.roll(jnp.where(lo_half, a1, a0), HEAD_DIM, axis=1)
    o_ref[0] = _head_pair_norm(num / den, g_ref[...], lane).astype(BF16)


def _fox_attention(main3, cp, cq, gain):
    b, s, _ = main3.shape
    tq = TQ_FOX
    npair = FOX_HEADS // 2
    qb, kb, vb = OFF_FOX // LANES, (OFF_FOX + D_FOX) // LANES, (OFF_FOX + 2 * D_FOX) // LANES
    return pl.pallas_call(
        functools.partial(_fox_kernel, tq=tq),
        grid=(b, npair, s // tq),
        in_specs=[
            pl.BlockSpec((1, tq, LANES), lambda bi, p, i: (bi, i, qb + p)),
            pl.BlockSpec((1, s, LANES), lambda bi, p, i: (bi, 0, kb + p)),
            pl.BlockSpec((1, s, LANES), lambda bi, p, i: (bi, 0, vb + p)),
            pl.BlockSpec((1, s, LANES), lambda bi, p, i: (bi, 0, 0)),
            pl.BlockSpec((1, 1, tq, 2), lambda bi, p, i: (bi, p, i, 0)),
            pl.BlockSpec((1, LANES), lambda bi, p, i: (0, p)),
        ],
        out_specs=pl.BlockSpec((1, tq, LANES), lambda bi, p, i: (bi, i, p)),
        out_shape=jax.ShapeDtypeStruct((b, s, D_FOX), BF16),
        scratch_shapes=[
            pltpu.VMEM((2, s, LANES), BF16),
            pltpu.VMEM((2, tq, LANES), F32),
            pltpu.VMEM((2, tq, LANES), F32),
        ],
        compiler_params=_cparams(("arbitrary", "arbitrary", "arbitrary")),
        name="fox_attn",
    )(main3, main3, main3, cp, cq, gain)


def _sb_kernel(q_ref, k_ref, v_ref, g_ref, o_ref, r_ref, acc_ref, *, tq, npairs):
    i = pl.program_id(2)
    lane = lax.broadcasted_iota(jnp.int32, (1, LANES), 1)
    qh = []
    for pr in range(npairs):
        q2 = q_ref[0, :, pr * LANES:(pr + 1) * LANES]
        zero = jnp.zeros_like(q2)
        qh += [jnp.where(lane < HEAD_DIM, q2, zero) * 0.125,
               jnp.where(lane < HEAD_DIM, zero, q2) * 0.125]
    row = lax.broadcasted_iota(jnp.int32, (tq, tq), 0)
    col = lax.broadcasted_iota(jnp.int32, (tq, tq), 1)
    later = jnp.where(row > col, 1.0, 0.0).astype(BF16)

    r_ref[...] = jnp.zeros_like(r_ref)
    acc_ref[...] = jnp.zeros_like(acc_ref)

    def tile(j, masked):
        start = pl.multiple_of(j * tq, tq)
        for hh in range(2 * npairs):
            ps = slice((hh // 2) * LANES, (hh // 2 + 1) * LANES)
            k_t = k_ref[0, pl.ds(start, tq), ps]
            v_t = v_ref[0, pl.ds(start, tq), ps]
            z = _dot_nt(qh[hh], k_t)
            log_beta = jnp.minimum(z, 0.0) - _softplus_neg_abs(z)
            log_keep = log_beta - z
            if masked:
                log_keep = jnp.where(col < row, log_keep, 0.0)
            suffix = _mask_dot_right(log_keep, later, 2)
            r_prev = r_ref[hh]

            def w_chunk(c):
                log_w = log_beta[:, c:c + LANES] + suffix[:, c:c + LANES] + r_prev
                if masked:
                    row_c = lax.broadcasted_iota(jnp.int32, (tq, LANES), 0)
                    col_c = lax.broadcasted_iota(jnp.int32, (tq, LANES), 1) + c
                    log_w = jnp.where(col_c < row_c, log_w, -jnp.inf)
                return jnp.exp(log_w).astype(BF16)

            w = jnp.concatenate([w_chunk(c) for c in range(0, tq, LANES)], axis=1)
            acc_ref[hh] = acc_ref[hh] + _dot(w, v_t)
            r_ref[hh] = r_prev + jnp.sum(log_keep, axis=-1, keepdims=True)

    tile(i, True)

    def cond(carry):
        it, live = carry
        return jnp.logical_and(it < i, live)

    def body(carry):
        it, _ = carry
        tile(i - 1 - it, False)
        return it + 1, jnp.max(r_ref[...]) > EXP_UNDERFLOW

    lax.while_loop(cond, body, (0, True))

    for pr in range(npairs):
        ps = slice(pr * LANES, (pr + 1) * LANES)
        o2 = jnp.where(lane < HEAD_DIM, acc_ref[2 * pr], acc_ref[2 * pr + 1])
        o_ref[0, :, ps] = _head_pair_norm(o2, g_ref[:, ps], lane).astype(BF16)


def _sb_attention(main3, gain):
    b, s, _ = main3.shape
    tq = TQ
    npairs = SB_PAIRS_PER_STEP
    w = npairs * LANES
    nsteps = D_SB // w
    qb, kb, vb = OFF_SB // w, (OFF_SB + D_SB) // w, (OFF_SB + 2 * D_SB) // w
    return pl.pallas_call(
        functools.partial(_sb_kernel, tq=tq, npairs=npairs),
        grid=(b, nsteps, s // tq),
        in_specs=[
            pl.BlockSpec((1, tq, w), lambda bi, p, i: (bi, i, qb + p)),
            pl.BlockSpec((1, s, w), lambda bi, p, i: (bi, 0, kb + p)),
            pl.BlockSpec((1, s, w), lambda bi, p, i: (bi, 0, vb + p)),
            pl.BlockSpec((1, w), lambda bi, p, i: (0, p)),
        ],
        out_specs=pl.BlockSpec((1, tq, w), lambda bi, p, i: (bi, i, p)),
        out_shape=jax.ShapeDtypeStruct((b, s, D_SB), BF16),
        scratch_shapes=[
            pltpu.VMEM((2 * npairs, tq, LANES), F32),
            pltpu.VMEM((2 * npairs, tq, LANES), F32),
        ],
        compiler_params=_cparams(("arbitrary", "arbitrary", "arbitrary")),
        name="sb_attn",
    )(main3, main3, main3, gain)


def _ssd_kernel(z_ref, xs_ref, bc_ref, dt_ref, cwx_ref, cbx_ref, cwb_ref, cbb_ref, dtb_ref,
                alog_ref, dexp_ref, ng_ref, e_ref, o_ref,
                extx_ref, extb_ref, state_ref):
    L = SSD_CHUNK
    hpg = SSD_HEADS // SSD_GROUPS
    gw = hpg * HEAD_DIM

    @pl.when(pl.program_id(1) == 0)
    def _():
        extx_ref[0:8, :] = jnp.zeros((8, D_SSD), F32)
        extb_ref[0:8, :] = jnp.zeros((8, 2 * D_BC), F32)
        state_ref[...] = jnp.zeros_like(state_ref)

    def conv_silu(raw_ref, ext_ref, w_ref, b_ref):
        raw = raw_ref[0].astype(F32)
        ext_ref[8:8 + L, :] = raw
        y = (b_ref[...] + w_ref[3:4, :] * raw
             + w_ref[2:3, :] * ext_ref[7:7 + L, :]
             + w_ref[1:2, :] * ext_ref[6:6 + L, :]
             + w_ref[0:1, :] * ext_ref[5:5 + L, :])
        ext_ref[0:8, :] = raw[L - 8:L, :]
        return y * _sigmoid(y)

    xs = conv_silu(xs_ref, extx_ref, cwx_ref, cbx_ref)
    bc = conv_silu(bc_ref, extb_ref, cwb_ref, cbb_ref)

    xdt_raw = dt_ref[0] + dtb_ref[...]
    dt = jnp.maximum(xdt_raw, 0.0) + _softplus_neg_abs(xdt_raw)
    a = -jnp.exp(alog_ref[...])
    da = dt * a
    row = lax.broadcasted_iota(jnp.int32, (L, L), 0)
    col = lax.broadcasted_iota(jnp.int32, (L, L), 1)
    causal = row >= col
    tri = jnp.where(causal, 1.0, 0.0).astype(BF16)
    a_cs = _mask_dot_left(tri, da, 3)
    a_cs_t = a_cs.T
    a_last = a_cs[L - 1:L, :]

    e01 = e_ref[...]
    dt_x = _mask_dot_right(dt, e01, 3)
    dec_in_x = _mask_dot_right(jnp.exp(a_cs), e01, 3)
    dec_end_x = _mask_dot_right(jnp.exp(a_last - a_cs), e01, 3)
    chunk_dec_x = _mask_dot_right(jnp.exp(a_last), e01, 3)

    xdt = xs * dt_x
    xdt_b = xdt.astype(BF16)
    xend_b = (xdt * dec_end_x).astype(BF16)

    lane = lax.broadcasted_iota(jnp.int32, (1, LANES), 1)
    ys = []
    for g in range(SSD_GROUPS):
        b_g = bc[:, g * SSD_STATE:(g + 1) * SSD_STATE].astype(BF16)
        c_g = bc[:, D_BC + g * SSD_STATE:D_BC + (g + 1) * SSD_STATE].astype(BF16)
        cb = _dot_nt(c_g, b_g)
        st = state_ref[g]
        y_off = _dot(c_g, st.astype(BF16)) * dec_in_x[:, g * gw:(g + 1) * gw]
        y_diag = []
        for pr in range(hpg // 2):
            outs = []
            for hh in range(2):
                h = g * hpg + 2 * pr + hh
                seg = jnp.exp(jnp.where(causal, a_cs[:, h:h + 1] - a_cs_t[h:h + 1, :], -jnp.inf))
                m = (cb * seg).astype(BF16)
                lo = (g * hpg + 2 * pr) * HEAD_DIM
                outs.append(_dot(m, xdt_b[:, lo:lo + LANES]))
            y_diag.append(jnp.where(lane < HEAD_DIM, outs[0], outs[1]))
        y_g = jnp.concatenate(y_diag, axis=-1) + y_off
        ys.append(y_g)
        state_ref[g] = (chunk_dec_x[:, g * gw:(g + 1) * gw] * st
                        + _dot_tn(b_g, xend_b[:, g * gw:(g + 1) * gw]))

    zf = z_ref[0].astype(F32)
    gate = zf * _sigmoid(zf)
    for g in range(SSD_GROUPS):
        sl = slice(g * gw, (g + 1) * gw)
        y_g = (ys[g] + xs[:, sl] * dexp_ref[:, sl]) * gate[:, sl]
        o_ref[0, :, sl] = _rms(y_g, ng_ref[:, sl]).astype(BF16)


def _ssd(main3, small3, cw, cb, dtb, alog, dexp, ng, e01):
    b, s, _ = main3.shape
    L = SSD_CHUNK
    zb, xb, bcb = OFF_Z // D_SSD, OFF_XS // D_SSD, OFF_BC // (2 * D_BC)
    cwx, cwb = cw[:, :D_SSD], cw[:, D_SSD:]
    cbx, cbb = cb[:, :D_SSD], cb[:, D_SSD:]
    const = lambda shape: pl.BlockSpec(shape, lambda bi, c: (0,) * len(shape))
    return pl.pallas_call(
        _ssd_kernel,
        grid=(b, s // L),
        in_specs=[
            pl.BlockSpec((1, L, D_SSD), lambda bi, c: (bi, c, zb)),
            pl.BlockSpec((1, L, D_SSD), lambda bi, c: (bi, c, xb)),
            pl.BlockSpec((1, L, 2 * D_BC), lambda bi, c: (bi, c, bcb)),
            pl.BlockSpec((1, L, LANES), lambda bi, c: (bi, c, 0)),
            const((SSD_CONV, D_SSD)), const((1, D_SSD)),
            const((SSD_CONV, 2 * D_BC)), const((1, 2 * D_BC)),
            const((1, LANES)), const((1, LANES)),
            const((1, D_SSD)), const((1, D_SSD)),
            const((LANES, D_SSD)),
        ],
        out_specs=pl.BlockSpec((1, L, D_SSD), lambda bi, c: (bi, c, 0)),
        out_shape=jax.ShapeDtypeStruct((b, s, D_SSD), BF16),
        scratch_shapes=[
            pltpu.VMEM((L + 8, D_SSD), F32),
            pltpu.VMEM((L + 8, 2 * D_BC), F32),
            pltpu.VMEM((SSD_GROUPS, SSD_STATE, D_SSD // SSD_GROUPS), F32),
        ],
        compiler_params=_cparams(("arbitrary", "arbitrary")),
        name="ssd",
    )(main3, main3, main3, small3, cwx, cbx, cwb, cbb, dtb, alog, dexp, ng, e01)


def _outproj_kernel(x_ref, yf_ref, ys_ref, yd_ref, wf_ref, ws_ref, wd_ref, o_ref):
    o_ref[...] = (x_ref[...] + _dot(yf_ref[...], wf_ref[...]) + _dot(ys_ref[...], ws_ref[...])
                  + _dot(yd_ref[...], wd_ref[...]))


def _outproj(x2, yf, ys, yd, wf, ws, wd):
    t = x2.shape[0]
    tm = TM_PROJ
    return pl.pallas_call(
        _outproj_kernel,
        grid=(t // tm,),
        in_specs=[
            pl.BlockSpec((tm, D_MODEL), lambda i: (i, 0)),
            pl.BlockSpec((tm, D_FOX), lambda i: (i, 0)),
            pl.BlockSpec((tm, D_SB), lambda i: (i, 0)),
            pl.BlockSpec((tm, D_SSD), lambda i: (i, 0)),
            pl.BlockSpec((D_FOX, D_MODEL), lambda i: (0, 0)),
            pl.BlockSpec((D_SB, D_MODEL), lambda i: (0, 0)),
            pl.BlockSpec((D_SSD, D_MODEL), lambda i: (0, 0)),
        ],
        out_specs=pl.BlockSpec((tm, D_MODEL), lambda i: (i, 0)),
        out_shape=jax.ShapeDtypeStruct((t, D_MODEL), F32),
        compiler_params=_cparams(("arbitrary",)),
        name="outproj",
    )(x2, yf, ys, yd, wf, ws, wd)


def _ffn_kernel(x_ref, g_ref, wu_ref, cw_ref, cb_ref, wd_ref, fg_ref, o_ref,
                carry_ref, ext_ref, acc_ref, *, tm, tiles_per_seq, final_norm):
    fc = FFN_FC
    cw2 = 2 * fc

    @pl.when(pl.program_id(0) % tiles_per_seq == 0)
    def _():
        carry_ref[...] = jnp.zeros_like(carry_ref)

    x = x_ref[...]
    h = _rms(x, g_ref[...]).astype(BF16)
    acc_ref[...] = x
    for c in range(D_FF // fc):
        cs = slice(c * cw2, (c + 1) * cw2)
        u = _dot(h, wu_ref[:, cs])
        ext_ref[0:8, :] = carry_ref[:, cs]
        ext_ref[8:8 + tm, :] = u
        carry_ref[:, cs] = u[tm - 8:tm, :]
        y = (cb_ref[:, cs] + cw_ref[2:3, cs] * u
             + cw_ref[1:2, cs] * ext_ref[7:7 + tm, :]
             + cw_ref[0:1, cs] * ext_ref[6:6 + tm, :])
        gate = y[:, :fc]
        act = (gate * _sigmoid(gate) * y[:, fc:]).astype(BF16)
        acc_ref[...] += _dot(act, wd_ref[c * fc:(c + 1) * fc, :])
    out = acc_ref[...]
    if final_norm:
        out = _rms(out, fg_ref[...])
    o_ref[...] = out


def _ffn(x2, g, wu, cw, cb, wd, fg, seq, final_norm):
    t = x2.shape[0]
    tm = TM_PROJ
    return pl.pallas_call(
        functools.partial(_ffn_kernel, tm=tm, tiles_per_seq=seq // tm, final_norm=final_norm),
        grid=(t // tm,),
        in_specs=[
            pl.BlockSpec((tm, D_MODEL), lambda i: (i, 0)),
            pl.BlockSpec((1, D_MODEL), lambda i: (0, 0)),
            pl.BlockSpec((D_MODEL, 2 * D_FF), lambda i: (0, 0)),
            pl.BlockSpec((FFN_CONV, 2 * D_FF), lambda i: (0, 0)),
            pl.BlockSpec((1, 2 * D_FF), lambda i: (0, 0)),
            pl.BlockSpec((D_FF, D_MODEL), lambda i: (0, 0)),
            pl.BlockSpec((1, D_MODEL), lambda i: (0, 0)),
        ],
        out_specs=pl.BlockSpec((tm, D_MODEL), lambda i: (i, 0)),
        out_shape=jax.ShapeDtypeStruct((t, D_MODEL), F32),
        scratch_shapes=[
            pltpu.VMEM((8, 2 * D_FF), F32),
            pltpu.VMEM((tm + 8, 2 * FFN_FC), F32),
            pltpu.VMEM((tm, D_MODEL), F32),
        ],
        compiler_params=_cparams(("arbitrary",)),
        name="ffn",
    )(x2, g, wu, cw, cb, wd, fg)


def _interleave_ff(a):
    lead = a.shape[:-1]
    n = D_FF // FFN_FC
    a = a.reshape(lead + (2, n, FFN_FC))
    a = jnp.swapaxes(a, -3, -2)
    return a.reshape(lead + (2 * D_FF,))


def _pad_lanes(v, offset):
    return jnp.zeros((1, LANES), F32).at[0, offset:offset + v.shape[0]].set(v.astype(F32))


def _layer(x2, b, s, mix_g, w_in, fox_f_bias, fox_out_g, sb_out_g, ssd_conv_w, ssd_conv_b,
           ssd_dt_bias, ssd_a_log, ssd_d, ssd_norm_g, w_out, ffn_g, w_up, ffn_conv_w, ffn_conv_b,
           w_down, final_g, final_norm):
    o = 0
    cols = {}
    for name, n in (("fq", D_FOX), ("fk", D_FOX), ("fv", D_FOX), ("ff", FOX_HEADS),
                    ("sq", D_SB), ("sk", D_SB), ("sv", D_SB),
                    ("z", D_SSD), ("xbc", D_SSD + 2 * D_BC), ("dt", SSD_HEADS)):
        cols[name] = w_in[:, o:o + n]
        o += n
    wm = jnp.concatenate([cols[k] for k in ("fq", "fk", "fv", "sq", "sk", "sv", "z", "xbc")],
                         axis=1).astype(BF16)
    ws = jnp.concatenate([cols["dt"], cols["ff"],
                          jnp.zeros((D_MODEL, LANES - SSD_HEADS - FOX_HEADS), F32)], axis=1).astype(BF16)

    main2, small2 = _inproj(x2, mix_g[None, :], wm, ws)
    main3 = main2.reshape(b, s, N_MAIN)
    small3 = small2.reshape(b, s, LANES)

    c_full, cp = _fox_gate(small3, _pad_lanes(fox_f_bias, SMALL_FF))
    c8 = c_full[:, :, SMALL_FF:SMALL_FF + FOX_HEADS]
    cq = c8.reshape(b, s, FOX_HEADS // 2, 2).transpose(0, 2, 1, 3)
    y_fox = _fox_attention(main3, cp, cq, fox_out_g[None, :])
    y_sb = _sb_attention(main3, sb_out_g[None, :])

    e01 = (jnp.arange(LANES)[:, None] == (jnp.arange(D_SSD)[None, :] // HEAD_DIM)).astype(BF16)
    y_ssd = _ssd(main3, small3, ssd_conv_w, ssd_conv_b[None, :],
                 _pad_lanes(ssd_dt_bias, SMALL_DT), _pad_lanes(ssd_a_log, SMALL_DT),
                 jnp.repeat(ssd_d, HEAD_DIM)[None, :], ssd_norm_g[None, :], e01)

    t = b * s
    wo = w_out.astype(BF16)
    x2 = _outproj(x2, y_fox.reshape(t, D_FOX), y_sb.reshape(t, D_SB), y_ssd.reshape(t, D_SSD),
                  wo[:D_FOX], wo[D_FOX:D_FOX + D_SB], wo[D_FOX + D_SB:])
    return _ffn(x2, ffn_g[None, :], _interleave_ff(w_up).astype(BF16), _interleave_ff(ffn_conv_w),
                _interleave_ff(ffn_conv_b)[None, :], w_down.astype(BF16), final_g[None, :], s,
                final_norm)


def kernel(x, mix_norm_g, w_in, fox_f_bias, fox_out_g, sb_out_g, ssd_conv_w, ssd_conv_b, ssd_dt_bias,
           ssd_a_log, ssd_d, ssd_norm_g, w_out, ffn_norm_g, w_up, ffn_conv_w, ffn_conv_b, w_down,
           final_norm_g):
    b, s, d = x.shape
    depth = w_in.shape[0]
    x2 = x.reshape(b * s, d)
    for l in range(depth):
        x2 = _layer(x2, b, s, mix_norm_g[l], w_in[l], fox_f_bias[l], fox_out_g[l], sb_out_g[l],
                    ssd_conv_w[l], ssd_conv_b[l], ssd_dt_bias[l], ssd_a_log[l], ssd_d[l],
                    ssd_norm_g[l], w_out[l], ffn_norm_g[l], w_up[l], ffn_conv_w[l], ffn_conv_b[l],
                    w_down[l], final_norm_g, l == depth - 1)
    return x2.reshape(b, s, d)
```

```python
import functools

import jax
import jax.numpy as jnp
from jax import lax
from jax.experimental import pallas as pl
from jax.experimental.pallas import tpu as pltpu

F32 = jnp.float32
BF16 = jnp.bfloat16

D_MODEL = 1024
HEAD_DIM = 64
FOX_HEADS = 8
SB_HEADS = 8
SSD_HEADS = 16
SSD_GROUPS = 2
SSD_STATE = 128
SSD_CONV = 4
SSD_CHUNK = 128
D_FOX = FOX_HEADS * HEAD_DIM
D_SB = SB_HEADS * HEAD_DIM
D_SSD = SSD_HEADS * HEAD_DIM
D_BC = SSD_GROUPS * SSD_STATE
D_FF = 2816
FFN_CONV = 3
NORM_EPS = 1e-6
CP_TERMS = 3
EXP_UNDERFLOW = -104.0

LANES = 128
N_MAIN = 3 * D_FOX + 3 * D_SB + D_SSD + D_SSD + 2 * D_BC
OFF_FOX = 0
OFF_SB = 3 * D_FOX
OFF_Z = OFF_SB + 3 * D_SB
OFF_XS = OFF_Z + D_SSD
OFF_BC = OFF_XS + D_SSD
SMALL_DT = 0
SMALL_FF = SSD_HEADS

VMEM_LIMIT = 56 * 1024 * 1024

TM_PROJ = 512
TQ = 256
TQ_FOX = 512
SB_PAIRS_PER_STEP = 4
FFN_FC = 256


def _cparams(sem):
    return pltpu.CompilerParams(dimension_semantics=sem, vmem_limit_bytes=VMEM_LIMIT)


def _split_bf16(x, n):
    parts, r = [], x
    for _ in range(n):
        p = r.astype(BF16)
        parts.append(p)
        r = r - p.astype(F32)
    return parts


def _dot(a, b):
    return jnp.dot(a, b, preferred_element_type=F32)


def _dot_nt(a, b):
    return lax.dot_general(a, b, (((1,), (1,)), ((), ())), preferred_element_type=F32)


def _dot_tn(a, b):
    return lax.dot_general(a, b, (((0,), (0,)), ((), ())), preferred_element_type=F32)


def _mask_dot_left(m01, x, n):
    out = None
    for p in _split_bf16(x, n):
        t = _dot(m01, p)
        out = t if out is None else out + t
    return out


def _mask_dot_right(x, m01, n):
    out = None
    for p in _split_bf16(x, n):
        t = _dot(p, m01)
        out = t if out is None else out + t
    return out


def _softplus_neg_abs(x):
    return jnp.log(1.0 + jnp.exp(-jnp.abs(x)))


def _sigmoid(x):
    return 1.0 / (1.0 + jnp.exp(-x))


def _rms(x, g):
    ms = jnp.mean(x * x, axis=-1, keepdims=True)
    return x * lax.rsqrt(ms + NORM_EPS) * g


def _head_pair_norm(o2, g, lane):
    lo = lane < HEAD_DIM
    sq = o2 * o2
    ms0 = jnp.sum(jnp.where(lo, sq, 0.0), axis=-1, keepdims=True) * (1.0 / HEAD_DIM)
    ms1 = jnp.sum(jnp.where(lo, 0.0, sq), axis=-1, keepdims=True) * (1.0 / HEAD_DIM)
    ms = jnp.where(lo, ms0, ms1)
    return o2 * lax.rsqrt(ms + NORM_EPS) * g


def _inproj_kernel(x_ref, g_ref, wm_ref, ws_ref, main_ref, small_ref, *, nc):
    h = _rms(x_ref[...], g_ref[...]).astype(BF16)
    for c in range(0, N_MAIN, nc):
        main_ref[:, c:c + nc] = _dot(h, wm_ref[:, c:c + nc]).astype(BF16)
    small_ref[...] = _dot(h, ws_ref[...])


def _inproj(x2, g, wm, ws):
    t = x2.shape[0]
    tm = TM_PROJ
    return pl.pallas_call(
        functools.partial(_inproj_kernel, nc=512),
        grid=(t // tm,),
        in_specs=[
            pl.BlockSpec((tm, D_MODEL), lambda i: (i, 0)),
            pl.BlockSpec((1, D_MODEL), lambda i: (0, 0)),
            pl.BlockSpec((D_MODEL, N_MAIN), lambda i: (0, 0)),
            pl.BlockSpec((D_MODEL, LANES), lambda i: (0, 0)),
        ],
        out_specs=[
            pl.BlockSpec((tm, N_MAIN), lambda i: (i, 0)),
            pl.BlockSpec((tm, LANES), lambda i: (i, 0)),
        ],
        out_shape=[
            jax.ShapeDtypeStruct((t, N_MAIN), BF16),
            jax.ShapeDtypeStruct((t, LANES), F32),
        ],
        compiler_params=_cparams(("arbitrary",)),
        name="inproj",
    )(x2, g, wm, ws)


def _fox_gate_kernel(s_ref, bias_ref, c_ref, cp_ref, carry_ref, *, tb):
    @pl.when(pl.program_id(1) == 0)
    def _():
        carry_ref[...] = jnp.zeros_like(carry_ref)

    xx = s_ref[0] + bias_ref[...]
    log_f = jnp.minimum(xx, 0.0) - _softplus_neg_abs(xx)
    row = lax.broadcasted_iota(jnp.int32, (tb, tb), 0)
    col = lax.broadcasted_iota(jnp.int32, (tb, tb), 1)
    tri = jnp.where(row >= col, 1.0, 0.0).astype(BF16)
    cum = _mask_dot_left(tri, log_f, 3) + carry_ref[...]
    c_ref[0] = cum
    carry_ref[...] = cum[tb - 1:tb, :]
    src = lax.broadcasted_iota(jnp.int32, (LANES, LANES), 0) - SMALL_FF
    dst = lax.broadcasted_iota(jnp.int32, (LANES, LANES), 1)
    head_ok = (src >= 0) & (src < FOX_HEADS)
    cp = None
    for j, part in enumerate(_split_bf16(-cum, CP_TERMS)):
        sel = jnp.where(head_ok & (dst == CP_TERMS * src + j), 1.0, 0.0).astype(BF16)
        t = _dot(part, sel)
        cp = t if cp is None else cp + t
    cp_ref[0] = cp.astype(BF16)


def _fox_gate(small3, bias_row):
    b, s, _ = small3.shape
    tb = 512
    blk = pl.BlockSpec((1, tb, LANES), lambda bi, i: (bi, i, 0))
    return pl.pallas_call(
        functools.partial(_fox_gate_kernel, tb=tb),
        grid=(b, s // tb),
        in_specs=[blk, pl.BlockSpec((1, LANES), lambda bi, i: (0, 0))],
        out_specs=[blk, blk],
        out_shape=[jax.ShapeDtypeStruct((b, s, LANES), F32),
                   jax.ShapeDtypeStruct((b, s, LANES), BF16)],
        scratch_shapes=[pltpu.VMEM((1, LANES), F32)],
        compiler_params=_cparams(("arbitrary", "arbitrary")),
        name="fox_gate",
    )(small3, bias_row)


def _fox_kernel(q_ref, k_ref, v_ref, cp_ref, cq_ref, g_ref, o_ref, vh_ref, m_ref, acc_ref, t_ref,
                p_ref, *, tq):
    p_idx = pl.program_id(1)
    i = pl.program_id(2)
    lane = lax.broadcasted_iota(jnp.int32, (1, LANES), 1)
    lo_half = lane < HEAD_DIM

    @pl.when(i == 0)
    def _():
        v2 = v_ref[0]
        one = jnp.ones_like(v2)
        vh_ref[0] = jnp.where(lo_half, v2, one)
        vh_ref[1] = jnp.where(lo_half, one, v2)

    q2 = q_ref[0]
    zero = jnp.zeros_like(q2)
    q_aug, c_t = [], []
    for hh in range(2):
        own = lo_half if hh == 0 else jnp.logical_not(lo_half)
        first = CP_TERMS * (2 * p_idx + hh)
        ones_at = jnp.where((lane >= first) & (lane < first + CP_TERMS), 1.0, 0.0).astype(BF16)
        q_aug.append(jnp.concatenate(
            [jnp.where(own, q2, zero) * 0.125, jnp.broadcast_to(ones_at, (tq, LANES))], axis=1))
        c_t.append(jnp.broadcast_to(cq_ref[0, 0, :, hh:hh + 1], (tq, LANES)))

    m_ref[...] = jnp.full_like(m_ref, -jnp.inf)
    acc_ref[...] = jnp.zeros_like(acc_ref)

    def tile(start, tk, masked):
        k_aug = jnp.concatenate([k_ref[0, pl.ds(start, tk), :], cp_ref[0, pl.ds(start, tk), :]], axis=1)
        for hh in range(2):
            t_ref[hh, :, 0:tk] = _dot_nt(q_aug[hh], k_aug)
        alpha = []
        for hh in range(2):
            t = t_ref[hh, :, 0:tk]
            if masked:
                row = lax.broadcasted_iota(jnp.int32, (tq, tk), 0)
                col = lax.broadcasted_iota(jnp.int32, (tq, tk), 1)
                t = jnp.where(row >= col, t, -jnp.inf)
            m_prev = m_ref[hh]
            m_new = jnp.maximum(m_prev, jnp.max(t, axis=-1, keepdims=True) + c_t[hh])
            shift = m_new - c_t[hh]
            for c in range(0, tk, LANES):
                p_ref[hh, :, c:c + LANES] = jnp.exp(t[:, c:c + LANES] - shift).astype(BF16)
            alpha.append(jnp.exp(m_prev - m_new))
            m_ref[hh] = m_new
        for hh in range(2):
            acc_ref[hh] = alpha[hh] * acc_ref[hh] + _dot(p_ref[hh, :, 0:tk], vh_ref[hh, pl.ds(start, tk), :])

    def body(j, carry):
        tile(pl.multiple_of(j * (2 * tq), 2 * tq), 2 * tq, False)
        return carry

    lax.fori_loop(0, i // 2, body, 0)

    @pl.when(i % 2 == 1)
    def _():
        tile(pl.multiple_of((i - 1) * tq, tq), tq, False)

    tile(pl.multiple_of(i * tq, tq), tq, True)

    a0, a1 = acc_ref[0], acc_ref[1]
    num = jnp.where(lo_half, a0, a1)
    den = pltpu.roll(jnp.where(lo_half, a1, a0), HEAD_DIM, axis=1)
    o_ref[0] = _head_pair_norm(num / den, g_ref[...], lane).astype(BF16)


def _fox_attention(main3, cp, cq, gain):
    b, s, _ = main3.shape
    tq = TQ_FOX
    npair = FOX_HEADS // 2
    qb, kb, vb = OFF_FOX // LANES, (OFF_FOX + D_FOX) // LANES, (OFF_FOX + 2 * D_FOX) // LANES
    return pl.pallas_call(
        functools.partial(_fox_kernel, tq=tq),
        grid=(b, npair, s // tq),
        in_specs=[
            pl.BlockSpec((1, tq, LANES), lambda bi, p, i: (bi, i, qb + p)),
            pl.BlockSpec((1, s, LANES), lambda bi, p, i: (bi, 0, kb + p)),
            pl.BlockSpec((1, s, LANES), lambda bi, p, i: (bi, 0, vb + p)),
            pl.BlockSpec((1, s, LANES), lambda bi, p, i: (bi, 0, 0)),
            pl.BlockSpec((1, 1, tq, 2), lambda bi, p, i: (bi, p, i, 0)),
            pl.BlockSpec((1, LANES), lambda bi, p, i: (0, p)),
        ],
        out_specs=pl.BlockSpec((1, tq, LANES), lambda bi, p, i: (bi, i, p)),
        out_shape=jax.ShapeDtypeStruct((b, s, D_FOX), BF16),
        scratch_shapes=[
            pltpu.VMEM((2, s, LANES), BF16),
            pltpu.VMEM((2, tq, LANES), F32),
            pltpu.VMEM((2, tq, LANES), F32),
            pltpu.VMEM((2, tq, 2 * tq), F32),
            pltpu.VMEM((2, tq, 2 * tq), BF16),
        ],
        compiler_params=_cparams(("arbitrary", "arbitrary", "arbitrary")),
        name="fox_attn",
    )(main3, main3, main3, cp, cq, gain)


def _sb_kernel(q_ref, k_ref, v_ref, g_ref, o_ref, r_ref, acc_ref, z_ref, lb_ref, hi_ref, lo_ref,
               w_ref, *, tq, npairs):
    i = pl.program_id(2)
    lane = lax.broadcasted_iota(jnp.int32, (1, LANES), 1)
    nheads = 2 * npairs
    qh = []
    for pr in range(npairs):
        q2 = q_ref[0, :, pr * LANES:(pr + 1) * LANES]
        zero = jnp.zeros_like(q2)
        qh += [jnp.where(lane < HEAD_DIM, q2, zero) * 0.125,
               jnp.where(lane < HEAD_DIM, zero, q2) * 0.125]
    row = lax.broadcasted_iota(jnp.int32, (tq, tq), 0)
    col = lax.broadcasted_iota(jnp.int32, (tq, tq), 1)
    later = jnp.where(row > col, 1.0, 0.0).astype(BF16)

    r_ref[...] = jnp.zeros_like(r_ref)
    acc_ref[...] = jnp.zeros_like(acc_ref)

    def tile(j, masked):
        start = pl.multiple_of(j * tq, tq)
        pair = lambda hh: slice((hh // 2) * LANES, (hh // 2 + 1) * LANES)
        for hh in range(nheads):
            z_ref[hh] = _dot_nt(qh[hh], k_ref[0, pl.ds(start, tq), pair(hh)])
        r_prev = []
        for hh in range(nheads):
            z = z_ref[hh]
            log_beta = jnp.minimum(z, 0.0) - _softplus_neg_abs(z)
            log_keep = log_beta - z
            if masked:
                log_keep = jnp.where(col < row, log_keep, 0.0)
            hi, lo = _split_bf16(log_keep, 2)
            lb_ref[hh] = log_beta
            hi_ref[hh] = hi
            lo_ref[hh] = lo
            r_prev.append(r_ref[hh])
            r_ref[hh] = r_prev[hh] + jnp.sum(log_keep, axis=-1, keepdims=True)
        for hh in range(nheads):
            z_ref[hh] = _dot(hi_ref[hh], later) + _dot(lo_ref[hh], later)
        for hh in range(nheads):
            log_w = lb_ref[hh] + z_ref[hh]
            chunks = []
            for c in range(0, tq, LANES):
                lw = log_w[:, c:c + LANES] + r_prev[hh]
                if masked:
                    row_c = lax.broadcasted_iota(jnp.int32, (tq, LANES), 0)
                    col_c = lax.broadcasted_iota(jnp.int32, (tq, LANES), 1) + c
                    lw = jnp.where(col_c < row_c, lw, -jnp.inf)
                chunks.append(jnp.exp(lw).astype(BF16))
            w_ref[hh] = jnp.concatenate(chunks, axis=1)
        for hh in range(nheads):
            acc_ref[hh] = acc_ref[hh] + _dot(w_ref[hh], v_ref[0, pl.ds(start, tq), pair(hh)])

    tile(i, True)

    def cond(carry):
        it, live = carry
        return jnp.logical_and(it < i, live)

    def body(carry):
        it, _ = carry
        tile(i - 1 - it, False)
        return it + 1, jnp.max(r_ref[...]) > EXP_UNDERFLOW

    lax.while_loop(cond, body, (0, True))

    for pr in range(npairs):
        ps = slice(pr * LANES, (pr + 1) * LANES)
        o2 = jnp.where(lane < HEAD_DIM, acc_ref[2 * pr], acc_ref[2 * pr + 1])
        o_ref[0, :, ps] = _head_pair_norm(o2, g_ref[:, ps], lane).astype(BF16)


def _sb_attention(main3, gain):
    b, s, _ = main3.shape
    tq = TQ
    npairs = SB_PAIRS_PER_STEP
    nheads = 2 * npairs
    w = npairs * LANES
    nsteps = D_SB // w
    qb, kb, vb = OFF_SB // w, (OFF_SB + D_SB) // w, (OFF_SB + 2 * D_SB) // w
    return pl.pallas_call(
        functools.partial(_sb_kernel, tq=tq, npairs=npairs),
        grid=(b, nsteps, s // tq),
        in_specs=[
            pl.BlockSpec((1, tq, w), lambda bi, p, i: (bi, i, qb + p)),
            pl.BlockSpec((1, s, w), lambda bi, p, i: (bi, 0, kb + p)),
            pl.BlockSpec((1, s, w), lambda bi, p, i: (bi, 0, vb + p)),
            pl.BlockSpec((1, w), lambda bi, p, i: (0, p)),
        ],
        out_specs=pl.BlockSpec((1, tq, w), lambda bi, p, i: (bi, i, p)),
        out_shape=jax.ShapeDtypeStruct((b, s, D_SB), BF16),
        scratch_shapes=[
            pltpu.VMEM((nheads, tq, LANES), F32),
            pltpu.VMEM((nheads, tq, LANES), F32),
            pltpu.VMEM((nheads, tq, tq), F32),
            pltpu.VMEM((nheads, tq, tq), F32),
            pltpu.VMEM((nheads, tq, tq), BF16),
            pltpu.VMEM((nheads, tq, tq), BF16),
            pltpu.VMEM((nheads, tq, tq), BF16),
        ],
        compiler_params=_cparams(("arbitrary", "arbitrary", "arbitrary")),
        name="sb_attn",
    )(main3, main3, main3, gain)


def _ssd_kernel(z_ref, xs_ref, bc_ref, dt_ref, cwx_ref, cbx_ref, cwb_ref, cbb_ref, dtb_ref,
                alog_ref, dexp_ref, ng_ref, e_ref, o_ref,
                extx_ref, extb_ref, state_ref):
    L = SSD_CHUNK
    hpg = SSD_HEADS // SSD_GROUPS
    gw = hpg * HEAD_DIM

    @pl.when(pl.program_id(1) == 0)
    def _():
        extx_ref[0:8, :] = jnp.zeros((8, D_SSD), F32)
        extb_ref[0:8, :] = jnp.zeros((8, 2 * D_BC), F32)
        state_ref[...] = jnp.zeros_like(state_ref)

    def conv_silu(raw_ref, ext_ref, w_ref, b_ref):
        raw = raw_ref[0].astype(F32)
        ext_ref[8:8 + L, :] = raw
        y = (b_ref[...] + w_ref[3:4, :] * raw
             + w_ref[2:3, :] * ext_ref[7:7 + L, :]
             + w_ref[1:2, :] * ext_ref[6:6 + L, :]
             + w_ref[0:1, :] * ext_ref[5:5 + L, :])
        ext_ref[0:8, :] = raw[L - 8:L, :]
        return y * _sigmoid(y)

    xs = conv_silu(xs_ref, extx_ref, cwx_ref, cbx_ref)
    bc = conv_silu(bc_ref, extb_ref, cwb_ref, cbb_ref)

    xdt_raw = dt_ref[0] + dtb_ref[...]
    dt = jnp.maximum(xdt_raw, 0.0) + _softplus_neg_abs(xdt_raw)
    a = -jnp.exp(alog_ref[...])
    da = dt * a
    row = lax.broadcasted_iota(jnp.int32, (L, L), 0)
    col = lax.broadcasted_iota(jnp.int32, (L, L), 1)
    causal = row >= col
    tri = jnp.where(causal, 1.0, 0.0).astype(BF16)
    a_cs = _mask_dot_left(tri, da, 3)
    a_cs_t = a_cs.T
    a_last = a_cs[L - 1:L, :]

    e01 = e_ref[...]
    dt_x = _mask_dot_right(dt, e01, 3)
    dec_in_x = _mask_dot_right(jnp.exp(a_cs), e01, 3)
    dec_end_x = _mask_dot_right(jnp.exp(a_last - a_cs), e01, 3)
    chunk_dec_x = _mask_dot_right(jnp.exp(a_last), e01, 3)

    xdt = xs * dt_x
    xdt_b = xdt.astype(BF16)
    xend_b = (xdt * dec_end_x).astype(BF16)

    lane = lax.broadcasted_iota(jnp.int32, (1, LANES), 1)
    ys = []
    for g in range(SSD_GROUPS):
        b_g = bc[:, g * SSD_STATE:(g + 1) * SSD_STATE].astype(BF16)
        c_g = bc[:, D_BC + g * SSD_STATE:D_BC + (g + 1) * SSD_STATE].astype(BF16)
        cb = _dot_nt(c_g, b_g)
        st = state_ref[g]
        y_off = _dot(c_g, st.astype(BF16)) * dec_in_x[:, g * gw:(g + 1) * gw]
        y_diag = []
        for pr in range(hpg // 2):
            outs = []
            for hh in range(2):
                h = g * hpg + 2 * pr + hh
                seg = jnp.exp(jnp.where(causal, a_cs[:, h:h + 1] - a_cs_t[h:h + 1, :], -jnp.inf))
                m = (cb * seg).astype(BF16)
                lo = (g * hpg + 2 * pr) * HEAD_DIM
                outs.append(_dot(m, xdt_b[:, lo:lo + LANES]))
            y_diag.append(jnp.where(lane < HEAD_DIM, outs[0], outs[1]))
        y_g = jnp.concatenate(y_diag, axis=-1) + y_off
        ys.append(y_g)
        state_ref[g] = (chunk_dec_x[:, g * gw:(g + 1) * gw] * st
                        + _dot_tn(b_g, xend_b[:, g * gw:(g + 1) * gw]))

    zf = z_ref[0].astype(F32)
    gate = zf * _sigmoid(zf)
    for g in range(SSD_GROUPS):
        sl = slice(g * gw, (g + 1) * gw)
        y_g = (ys[g] + xs[:, sl] * dexp_ref[:, sl]) * gate[:, sl]
        o_ref[0, :, sl] = _rms(y_g, ng_ref[:, sl]).astype(BF16)


def _ssd(main3, small3, cw, cb, dtb, alog, dexp, ng, e01):
    b, s, _ = main3.shape
    L = SSD_CHUNK
    zb, xb, bcb = OFF_Z // D_SSD, OFF_XS // D_SSD, OFF_BC // (2 * D_BC)
    cwx, cwb = cw[:, :D_SSD], cw[:, D_SSD:]
    cbx, cbb = cb[:, :D_SSD], cb[:, D_SSD:]
    const = lambda shape: pl.BlockSpec(shape, lambda bi, c: (0,) * len(shape))
    return pl.pallas_call(
        _ssd_kernel,
        grid=(b, s // L),
        in_specs=[
            pl.BlockSpec((1, L, D_SSD), lambda bi, c: (bi, c, zb)),
            pl.BlockSpec((1, L, D_SSD), lambda bi, c: (bi, c, xb)),
            pl.BlockSpec((1, L, 2 * D_BC), lambda bi, c: (bi, c, bcb)),
            pl.BlockSpec((1, L, LANES), lambda bi, c: (bi, c, 0)),
            const((SSD_CONV, D_SSD)), const((1, D_SSD)),
            const((SSD_CONV, 2 * D_BC)), const((1, 2 * D_BC)),
            const((1, LANES)), const((1, LANES)),
            const((1, D_SSD)), const((1, D_SSD)),
            const((LANES, D_SSD)),
        ],
        out_specs=pl.BlockSpec((1, L, D_SSD), lambda bi, c: (bi, c, 0)),
        out_shape=jax.ShapeDtypeStruct((b, s, D_SSD), BF16),
        scratch_shapes=[
            pltpu.VMEM((L + 8, D_SSD), F32),
            pltpu.VMEM((L + 8, 2 * D_BC), F32),
            pltpu.VMEM((SSD_GROUPS, SSD_STATE, D_SSD // SSD_GROUPS), F32),
        ],
        compiler_params=_cparams(("arbitrary", "arbitrary")),
        name="ssd",
    )(main3, main3, main3, small3, cwx, cbx, cwb, cbb, dtb, alog, dexp, ng, e01)


def _outproj_kernel(x_ref, yf_ref, ys_ref, yd_ref, wf_ref, ws_ref, wd_ref, o_ref):
    o_ref[...] = (x_ref[...] + _dot(yf_ref[...], wf_ref[...]) + _dot(ys_ref[...], ws_ref[...])
                  + _dot(yd_ref[...], wd_ref[...]))


def _outproj(x2, yf, ys, yd, wf, ws, wd):
    t = x2.shape[0]
    tm = TM_PROJ
    return pl.pallas_call(
        _outproj_kernel,
        grid=(t // tm,),
        in_specs=[
            pl.BlockSpec((tm, D_MODEL), lambda i: (i, 0)),
            pl.BlockSpec((tm, D_FOX), lambda i: (i, 0)),
            pl.BlockSpec((tm, D_SB), lambda i: (i, 0)),
            pl.BlockSpec((tm, D_SSD), lambda i: (i, 0)),
            pl.BlockSpec((D_FOX, D_MODEL), lambda i: (0, 0)),
            pl.BlockSpec((D_SB, D_MODEL), lambda i: (0, 0)),
            pl.BlockSpec((D_SSD, D_MODEL), lambda i: (0, 0)),
        ],
        out_specs=pl.BlockSpec((tm, D_MODEL), lambda i: (i, 0)),
        out_shape=jax.ShapeDtypeStruct((t, D_MODEL), F32),
        compiler_params=_cparams(("arbitrary",)),
        name="outproj",
    )(x2, yf, ys, yd, wf, ws, wd)


def _ffn_kernel(x_ref, g_ref, wu_ref, cw_ref, cb_ref, wd_ref, fg_ref, o_ref,
                carry_ref, ext_ref, acc_ref, *, tm, tiles_per_seq, final_norm):
    fc = FFN_FC
    cw2 = 2 * fc

    @pl.when(pl.program_id(0) % tiles_per_seq == 0)
    def _():
        carry_ref[...] = jnp.zeros_like(carry_ref)

    x = x_ref[...]
    h = _rms(x, g_ref[...]).astype(BF16)
    acc_ref[...] = x
    for c in range(D_FF // fc):
        cs = slice(c * cw2, (c + 1) * cw2)
        u = _dot(h, wu_ref[:, cs])
        ext_ref[0:8, :] = carry_ref[:, cs]
        ext_ref[8:8 + tm, :] = u
        carry_ref[:, cs] = u[tm - 8:tm, :]
        y = (cb_ref[:, cs] + cw_ref[2:3, cs] * u
             + cw_ref[1:2, cs] * ext_ref[7:7 + tm, :]
             + cw_ref[0:1, cs] * ext_ref[6:6 + tm, :])
        gate = y[:, :fc]
        act = (gate * _sigmoid(gate) * y[:, fc:]).astype(BF16)
        acc_ref[...] += _dot(act, wd_ref[c * fc:(c + 1) * fc, :])
    out = acc_ref[...]
    if final_norm:
        out = _rms(out, fg_ref[...])
    o_ref[...] = out


def _ffn(x2, g, wu, cw, cb, wd, fg, seq, final_norm):
    t = x2.shape[0]
    tm = TM_PROJ
    return pl.pallas_call(
        functools.partial(_ffn_kernel, tm=tm, tiles_per_seq=seq // tm, final_norm=final_norm),
        grid=(t // tm,),
        in_specs=[
            pl.BlockSpec((tm, D_MODEL), lambda i: (i, 0)),
            pl.BlockSpec((1, D_MODEL), lambda i: (0, 0)),
            pl.BlockSpec((D_MODEL, 2 * D_FF), lambda i: (0, 0)),
            pl.BlockSpec((FFN_CONV, 2 * D_FF), lambda i: (0, 0)),
            pl.BlockSpec((1, 2 * D_FF), lambda i: (0, 0)),
            pl.BlockSpec((D_FF, D_MODEL), lambda i: (0, 0)),
            pl.BlockSpec((1, D_MODEL), lambda i: (0, 0)),
        ],
        out_specs=pl.BlockSpec((tm, D_MODEL), lambda i: (i, 0)),
        out_shape=jax.ShapeDtypeStruct((t, D_MODEL), F32),
        scratch_shapes=[
            pltpu.VMEM((8, 2 * D_FF), F32),
            pltpu.VMEM((tm + 8, 2 * FFN_FC), F32),
            pltpu.VMEM((tm, D_MODEL), F32),
        ],
        compiler_params=_cparams(("arbitrary",)),
        name="ffn",
    )(x2, g, wu, cw, cb, wd, fg)


def _interleave_ff(a):
    lead = a.shape[:-1]
    n = D_FF // FFN_FC
    a = a.reshape(lead + (2, n, FFN_FC))
    a = jnp.swapaxes(a, -3, -2)
    return a.reshape(lead + (2 * D_FF,))


def _pad_lanes(v, offset):
    return jnp.zeros((1, LANES), F32).at[0, offset:offset + v.shape[0]].set(v.astype(F32))


def _layer(x2, b, s, mix_g, w_in, fox_f_bias, fox_out_g, sb_out_g, ssd_conv_w, ssd_conv_b,
           ssd_dt_bias, ssd_a_log, ssd_d, ssd_norm_g, w_out, ffn_g, w_up, ffn_conv_w, ffn_conv_b,
           w_down, final_g, final_norm):
    o = 0
    cols = {}
    for name, n in (("fq", D_FOX), ("fk", D_FOX), ("fv", D_FOX), ("ff", FOX_HEADS),
                    ("sq", D_SB), ("sk", D_SB), ("sv", D_SB),
                    ("z", D_SSD), ("xbc", D_SSD + 2 * D_BC), ("dt", SSD_HEADS)):
        cols[name] = w_in[:, o:o + n]
        o += n
    wm = jnp.concatenate([cols[k] for k in ("fq", "fk", "fv", "sq", "sk", "sv", "z", "xbc")],
                         axis=1).astype(BF16)
    ws = jnp.concatenate([cols["dt"], cols["ff"],
                          jnp.zeros((D_MODEL, LANES - SSD_HEADS - FOX_HEADS), F32)], axis=1).astype(BF16)

    main2, small2 = _inproj(x2, mix_g[None, :], wm, ws)
    main3 = main2.reshape(b, s, N_MAIN)
    small3 = small2.reshape(b, s, LANES)

    c_full, cp = _fox_gate(small3, _pad_lanes(fox_f_bias, SMALL_FF))
    c8 = c_full[:, :, SMALL_FF:SMALL_FF + FOX_HEADS]
    cq = c8.reshape(b, s, FOX_HEADS // 2, 2).transpose(0, 2, 1, 3)
    y_fox = _fox_attention(main3, cp, cq, fox_out_g[None, :])
    y_sb = _sb_attention(main3, sb_out_g[None, :])

    e01 = (jnp.arange(LANES)[:, None] == (jnp.arange(D_SSD)[None, :] // HEAD_DIM)).astype(BF16)
    y_ssd = _ssd(main3, small3, ssd_conv_w, ssd_conv_b[None, :],
                 _pad_lanes(ssd_dt_bias, SMALL_DT), _pad_lanes(ssd_a_log, SMALL_DT),
                 jnp.repeat(ssd_d, HEAD_DIM)[None, :], ssd_norm_g[None, :], e01)

    t = b * s
    wo = w_out.astype(BF16)
    x2 = _outproj(x2, y_fox.reshape(t, D_FOX), y_sb.reshape(t, D_SB), y_ssd.reshape(t, D_SSD),
                  wo[:D_FOX], wo[D_FOX:D_FOX + D_SB], wo[D_FOX + D_SB:])
    return _ffn(x2, ffn_g[None, :], _interleave_ff(w_up).astype(BF16), _interleave_ff(ffn_conv_w),
                _interleave_ff(ffn_conv_b)[None, :], w_down.astype(BF16), final_g[None, :], s,
                final_norm)


def kernel(x, mix_norm_g, w_in, fox_f_bias, fox_out_g, sb_out_g, ssd_conv_w, ssd_conv_b, ssd_dt_bias,
           ssd_a_log, ssd_d, ssd_norm_g, w_out, ffn_norm_g, w_up, ffn_conv_w, ffn_conv_b, w_down,
           final_norm_g):
    b, s, d = x.shape
    depth = w_in.shape[0]
    x2 = x.reshape(b * s, d)
    for l in range(depth):
        x2 = _layer(x2, b, s, mix_norm_g[l], w_in[l], fox_f_bias[l], fox_out_g[l], sb_out_g[l],
                    ssd_conv_w[l], ssd_conv_b[l], ssd_dt_bias[l], ssd_a_log[l], ssd_d[l],
                    ssd_norm_g[l], w_out[l], ffn_norm_g[l], w_up[l], ffn_conv_w[l], ffn_conv_b[l],
                    w_down[l], final_norm_g, l == depth - 1)
    return x2.reshape(b, s, d)
```

```python
import functools

import jax
import jax.numpy as jnp
from jax import lax
from jax.experimental import pallas as pl
from jax.experimental.pallas import tpu as pltpu

F32 = jnp.float32
BF16 = jnp.bfloat16

D_MODEL = 1024
HEAD_DIM = 64
FOX_HEADS = 8
SB_HEADS = 8
SSD_HEADS = 16
SSD_GROUPS = 2
SSD_STATE = 128
SSD_CONV = 4
SSD_CHUNK = 128
D_FOX = FOX_HEADS * HEAD_DIM
D_SB = SB_HEADS * HEAD_DIM
D_SSD = SSD_HEADS * HEAD_DIM
D_BC = SSD_GROUPS * SSD_STATE
D_FF = 2816
FFN_CONV = 3
NORM_EPS = 1e-6
CP_TERMS = 3
EXP_UNDERFLOW = -104.0

LANES = 128
N_MAIN = 3 * D_FOX + 3 * D_SB + D_SSD + D_SSD + 2 * D_BC
OFF_FOX = 0
OFF_SB = 3 * D_FOX
OFF_Z = OFF_SB + 3 * D_SB
OFF_XS = OFF_Z + D_SSD
OFF_BC = OFF_XS + D_SSD
SMALL_DT = 0
SMALL_FF = SSD_HEADS

VMEM_LIMIT = 56 * 1024 * 1024

TM_PROJ = 512
TQ = 256
TQ_FOX = 512
ROW_BLOCK = 32
FOX_PAIRS_PER_STEP = 2
SB_PAIRS_PER_STEP = 4
FFN_FC = 256


def _cparams(sem):
    return pltpu.CompilerParams(dimension_semantics=sem, vmem_limit_bytes=VMEM_LIMIT)


def _split_bf16(x, n):
    parts, r = [], x
    for _ in range(n):
        p = r.astype(BF16)
        parts.append(p)
        r = r - p.astype(F32)
    return parts


def _dot(a, b):
    return jnp.dot(a, b, preferred_element_type=F32)


def _dot_nt(a, b):
    return lax.dot_general(a, b, (((1,), (1,)), ((), ())), preferred_element_type=F32)


def _dot_tn(a, b):
    return lax.dot_general(a, b, (((0,), (0,)), ((), ())), preferred_element_type=F32)


def _mask_dot_left(m01, x, n):
    out = None
    for p in _split_bf16(x, n):
        t = _dot(m01, p)
        out = t if out is None else out + t
    return out


def _mask_dot_right(x, m01, n):
    out = None
    for p in _split_bf16(x, n):
        t = _dot(p, m01)
        out = t if out is None else out + t
    return out


def _softplus_neg_abs(x):
    return jnp.log(1.0 + jnp.exp(-jnp.abs(x)))


def _sigmoid(x):
    return 1.0 / (1.0 + jnp.exp(-x))


def _rms(x, g):
    ms = jnp.mean(x * x, axis=-1, keepdims=True)
    return x * lax.rsqrt(ms + NORM_EPS) * g


def _head_pair_norm(o2, g, lane):
    lo = lane < HEAD_DIM
    sq = o2 * o2
    ms0 = jnp.sum(jnp.where(lo, sq, 0.0), axis=-1, keepdims=True) * (1.0 / HEAD_DIM)
    ms1 = jnp.sum(jnp.where(lo, 0.0, sq), axis=-1, keepdims=True) * (1.0 / HEAD_DIM)
    ms = jnp.where(lo, ms0, ms1)
    return o2 * lax.rsqrt(ms + NORM_EPS) * g


def _inproj_kernel(x_ref, g_ref, wm_ref, ws_ref, main_ref, small_ref, *, nc):
    h = _rms(x_ref[...], g_ref[...]).astype(BF16)
    for c in range(0, N_MAIN, nc):
        main_ref[:, c:c + nc] = _dot(h, wm_ref[:, c:c + nc]).astype(BF16)
    small_ref[...] = _dot(h, ws_ref[...])


def _inproj(x2, g, wm, ws):
    t = x2.shape[0]
    tm = TM_PROJ
    return pl.pallas_call(
        functools.partial(_inproj_kernel, nc=512),
        grid=(t // tm,),
        in_specs=[
            pl.BlockSpec((tm, D_MODEL), lambda i: (i, 0)),
            pl.BlockSpec((1, D_MODEL), lambda i: (0, 0)),
            pl.BlockSpec((D_MODEL, N_MAIN), lambda i: (0, 0), pipeline_mode=pl.Buffered(1)),
            pl.BlockSpec((D_MODEL, LANES), lambda i: (0, 0), pipeline_mode=pl.Buffered(1)),
        ],
        out_specs=[
            pl.BlockSpec((tm, N_MAIN), lambda i: (i, 0)),
            pl.BlockSpec((tm, LANES), lambda i: (i, 0)),
        ],
        out_shape=[
            jax.ShapeDtypeStruct((t, N_MAIN), BF16),
            jax.ShapeDtypeStruct((t, LANES), F32),
        ],
        compiler_params=_cparams(("arbitrary",)),
        name="inproj",
    )(x2, g, wm, ws)


def _fox_gate_kernel(s_ref, bias_ref, c_ref, cp_ref, carry_ref, *, tb):
    @pl.when(pl.program_id(1) == 0)
    def _():
        carry_ref[...] = jnp.zeros_like(carry_ref)

    xx = s_ref[0] + bias_ref[...]
    log_f = jnp.minimum(xx, 0.0) - _softplus_neg_abs(xx)
    row = lax.broadcasted_iota(jnp.int32, (tb, tb), 0)
    col = lax.broadcasted_iota(jnp.int32, (tb, tb), 1)
    tri = jnp.where(row >= col, 1.0, 0.0).astype(BF16)
    cum = _mask_dot_left(tri, log_f, 3) + carry_ref[...]
    c_ref[0] = cum
    carry_ref[...] = cum[tb - 1:tb, :]
    src = lax.broadcasted_iota(jnp.int32, (LANES, LANES), 0) - SMALL_FF
    dst = lax.broadcasted_iota(jnp.int32, (LANES, LANES), 1)
    head_ok = (src >= 0) & (src < FOX_HEADS)
    cp = None
    for j, part in enumerate(_split_bf16(-cum, CP_TERMS)):
        sel = jnp.where(head_ok & (dst == CP_TERMS * src + j), 1.0, 0.0).astype(BF16)
        t = _dot(part, sel)
        cp = t if cp is None else cp + t
    cp_ref[0] = cp.astype(BF16)


def _fox_gate(small3, bias_row):
    b, s, _ = small3.shape
    tb = 512
    blk = pl.BlockSpec((1, tb, LANES), lambda bi, i: (bi, i, 0))
    return pl.pallas_call(
        functools.partial(_fox_gate_kernel, tb=tb),
        grid=(b, s // tb),
        in_specs=[blk, pl.BlockSpec((1, LANES), lambda bi, i: (0, 0))],
        out_specs=[blk, blk],
        out_shape=[jax.ShapeDtypeStruct((b, s, LANES), F32),
                   jax.ShapeDtypeStruct((b, s, LANES), BF16)],
        scratch_shapes=[pltpu.VMEM((1, LANES), F32)],
        compiler_params=_cparams(("arbitrary", "arbitrary")),
        name="fox_gate",
    )(small3, bias_row)


def _fox_kernel(q_ref, k_ref, v_ref, cp_ref, cq_ref, g_ref, o_ref, vh_ref, m_ref, acc_ref, c_ref,
                alpha_ref, t_ref, p_ref, *, tq, npairs):
    p_idx = pl.program_id(1)
    i = pl.program_id(2)
    lane = lax.broadcasted_iota(jnp.int32, (1, LANES), 1)
    lo_half = lane < HEAD_DIM
    nheads = 2 * npairs
    pair = lambda h: slice((h // 2) * LANES, (h // 2 + 1) * LANES)

    @pl.when(i == 0)
    def _():
        for pr in range(npairs):
            v2 = v_ref[0, :, pair(2 * pr)]
            one = jnp.ones_like(v2)
            vh_ref[2 * pr] = jnp.where(lo_half, v2, one)
            vh_ref[2 * pr + 1] = jnp.where(lo_half, one, v2)

    q_aug = []
    for h in range(nheads):
        q2 = q_ref[0, :, pair(h)]
        own = lo_half if h % 2 == 0 else jnp.logical_not(lo_half)
        first = CP_TERMS * (nheads * p_idx + h)
        ones_at = jnp.where((lane >= first) & (lane < first + CP_TERMS), 1.0, 0.0).astype(BF16)
        q_aug.append(jnp.concatenate(
            [jnp.where(own, q2, jnp.zeros_like(q2)) * 0.125, jnp.broadcast_to(ones_at, (tq, LANES))],
            axis=1))
        c_ref[h] = jnp.broadcast_to(cq_ref[0, 0, :, h:h + 1], (tq, LANES))

    m_ref[...] = jnp.full_like(m_ref, -jnp.inf)
    acc_ref[...] = jnp.zeros_like(acc_ref)

    def tile(start, tk, masked):
        cp_t = cp_ref[0, pl.ds(start, tk), :]
        for h in range(nheads):
            k_aug = jnp.concatenate([k_ref[0, pl.ds(start, tk), pair(h)], cp_t], axis=1)
            t_ref[h, :, 0:tk] = _dot_nt(q_aug[h], k_aug)
        for h in range(nheads):
            for r in range(0, tq, ROW_BLOCK):
                rows = slice(r, r + ROW_BLOCK)
                t = t_ref[h, rows, 0:tk]
                if masked:
                    row = lax.broadcasted_iota(jnp.int32, (ROW_BLOCK, tk), 0) + r
                    col = lax.broadcasted_iota(jnp.int32, (ROW_BLOCK, tk), 1)
                    t = jnp.where(row >= col, t, -jnp.inf)
                m_prev = m_ref[h, rows, :]
                m_new = jnp.maximum(m_prev, jnp.max(t, axis=-1, keepdims=True) + c_ref[h, rows, :])
                shift = m_new - c_ref[h, rows, :]
                for c in range(0, tk, LANES):
                    p_ref[h, rows, c:c + LANES] = jnp.exp(t[:, c:c + LANES] - shift).astype(BF16)
                alpha_ref[h, rows, :] = jnp.exp(m_prev - m_new)
                m_ref[h, rows, :] = m_new
        for h in range(nheads):
            acc_ref[h] = alpha_ref[h] * acc_ref[h] + _dot(p_ref[h, :, 0:tk], vh_ref[h, pl.ds(start, tk), :])

    def body(j, carry):
        tile(pl.multiple_of(j * (2 * tq), 2 * tq), 2 * tq, False)
        return carry

    lax.fori_loop(0, i // 2, body, 0)

    @pl.when(i % 2 == 1)
    def _():
        tile(pl.multiple_of((i - 1) * tq, tq), tq, False)

    tile(pl.multiple_of(i * tq, tq), tq, True)

    for pr in range(npairs):
        a0, a1 = acc_ref[2 * pr], acc_ref[2 * pr + 1]
        num = jnp.where(lo_half, a0, a1)
        den = pltpu.roll(jnp.where(lo_half, a1, a0), HEAD_DIM, axis=1)
        o_ref[0, :, pair(2 * pr)] = _head_pair_norm(num / den, g_ref[:, pair(2 * pr)], lane).astype(BF16)


def _fox_attention(main3, cp, cq, gain):
    b, s, _ = main3.shape
    tq = TQ_FOX
    npairs = FOX_PAIRS_PER_STEP
    nheads = 2 * npairs
    w = npairs * LANES
    nsteps = D_FOX // w
    qb, kb, vb = OFF_FOX // w, (OFF_FOX + D_FOX) // w, (OFF_FOX + 2 * D_FOX) // w
    return pl.pallas_call(
        functools.partial(_fox_kernel, tq=tq, npairs=npairs),
        grid=(b, nsteps, s // tq),
        in_specs=[
            pl.BlockSpec((1, tq, w), lambda bi, p, i: (bi, i, qb + p)),
            pl.BlockSpec((1, s, w), lambda bi, p, i: (bi, 0, kb + p)),
            pl.BlockSpec((1, s, w), lambda bi, p, i: (bi, 0, vb + p)),
            pl.BlockSpec((1, s, LANES), lambda bi, p, i: (bi, 0, 0)),
            pl.BlockSpec((1, 1, tq, nheads), lambda bi, p, i: (bi, p, i, 0)),
            pl.BlockSpec((1, w), lambda bi, p, i: (0, p)),
        ],
        out_specs=pl.BlockSpec((1, tq, w), lambda bi, p, i: (bi, i, p)),
        out_shape=jax.ShapeDtypeStruct((b, s, D_FOX), BF16),
        scratch_shapes=[
            pltpu.VMEM((nheads, s, LANES), BF16),
            pltpu.VMEM((nheads, tq, LANES), F32),
            pltpu.VMEM((nheads, tq, LANES), F32),
            pltpu.VMEM((nheads, tq, LANES), F32),
            pltpu.VMEM((nheads, tq, LANES), F32),
            pltpu.VMEM((nheads, tq, 2 * tq), F32),
            pltpu.VMEM((nheads, tq, 2 * tq), BF16),
        ],
        compiler_params=_cparams(("arbitrary", "arbitrary", "arbitrary")),
        name="fox_attn",
    )(main3, main3, main3, cp, cq, gain)


def _sb_kernel(q_ref, k_ref, v_ref, g_ref, o_ref, r_ref, acc_ref, z_ref, lb_ref, lk_ref, w_ref, *,
               tq, npairs):
    i = pl.program_id(2)
    lane = lax.broadcasted_iota(jnp.int32, (1, LANES), 1)
    nheads = 2 * npairs
    qh = []
    for pr in range(npairs):
        q2 = q_ref[0, :, pr * LANES:(pr + 1) * LANES]
        zero = jnp.zeros_like(q2)
        qh += [jnp.where(lane < HEAD_DIM, q2, zero) * 0.125,
               jnp.where(lane < HEAD_DIM, zero, q2) * 0.125]
    row = lax.broadcasted_iota(jnp.int32, (tq, tq), 0)
    col = lax.broadcasted_iota(jnp.int32, (tq, tq), 1)
    later = jnp.where(row > col, 1.0, 0.0).astype(BF16)

    r_ref[...] = jnp.zeros_like(r_ref)
    acc_ref[...] = jnp.zeros_like(acc_ref)

    def tile(j, masked):
        start = pl.multiple_of(j * tq, tq)
        pair = lambda hh: slice((hh // 2) * LANES, (hh // 2 + 1) * LANES)
        for hh in range(nheads):
            z_ref[hh] = _dot_nt(qh[hh], k_ref[0, pl.ds(start, tq), pair(hh)])
        r_prev = []
        for hh in range(nheads):
            z = z_ref[hh]
            log_beta = jnp.minimum(z, 0.0) - _softplus_neg_abs(z)
            log_keep = log_beta - z
            if masked:
                log_keep = jnp.where(col < row, log_keep, 0.0)
            lb_ref[hh] = log_beta
            lk_ref[hh] = log_keep.astype(BF16)
            r_prev.append(r_ref[hh])
            r_ref[hh] = r_prev[hh] + jnp.sum(log_keep, axis=-1, keepdims=True)
        for hh in range(nheads):
            z_ref[hh] = _dot(lk_ref[hh], later)
        for hh in range(nheads):
            log_w = lb_ref[hh] + z_ref[hh]
            chunks = []
            for c in range(0, tq, LANES):
                lw = log_w[:, c:c + LANES] + r_prev[hh]
                if masked:
                    row_c = lax.broadcasted_iota(jnp.int32, (tq, LANES), 0)
                    col_c = lax.broadcasted_iota(jnp.int32, (tq, LANES), 1) + c
                    lw = jnp.where(col_c < row_c, lw, -jnp.inf)
                chunks.append(jnp.exp(lw).astype(BF16))
            w_ref[hh] = jnp.concatenate(chunks, axis=1)
        for hh in range(nheads):
            acc_ref[hh] = acc_ref[hh] + _dot(w_ref[hh], v_ref[0, pl.ds(start, tq), pair(hh)])

    tile(i, True)

    def cond(carry):
        it, live = carry
        return jnp.logical_and(it < i, live)

    def body(carry):
        it, _ = carry
        tile(i - 1 - it, False)
        return it + 1, jnp.max(r_ref[...]) > EXP_UNDERFLOW

    lax.while_loop(cond, body, (0, True))

    for pr in range(npairs):
        ps = slice(pr * LANES, (pr + 1) * LANES)
        o2 = jnp.where(lane < HEAD_DIM, acc_ref[2 * pr], acc_ref[2 * pr + 1])
        o_ref[0, :, ps] = _head_pair_norm(o2, g_ref[:, ps], lane).astype(BF16)


def _sb_attention(main3, gain):
    b, s, _ = main3.shape
    tq = TQ
    npairs = SB_PAIRS_PER_STEP
    nheads = 2 * npairs
    w = npairs * LANES
    nsteps = D_SB // w
    qb, kb, vb = OFF_SB // w, (OFF_SB + D_SB) // w, (OFF_SB + 2 * D_SB) // w
    return pl.pallas_call(
        functools.partial(_sb_kernel, tq=tq, npairs=npairs),
        grid=(b, nsteps, s // tq),
        in_specs=[
            pl.BlockSpec((1, tq, w), lambda bi, p, i: (bi, i, qb + p)),
            pl.BlockSpec((1, s, w), lambda bi, p, i: (bi, 0, kb + p)),
            pl.BlockSpec((1, s, w), lambda bi, p, i: (bi, 0, vb + p)),
            pl.BlockSpec((1, w), lambda bi, p, i: (0, p)),
        ],
        out_specs=pl.BlockSpec((1, tq, w), lambda bi, p, i: (bi, i, p)),
        out_shape=jax.ShapeDtypeStruct((b, s, D_SB), BF16),
        scratch_shapes=[
            pltpu.VMEM((nheads, tq, LANES), F32),
            pltpu.VMEM((nheads, tq, LANES), F32),
            pltpu.VMEM((nheads, tq, tq), F32),
            pltpu.VMEM((nheads, tq, tq), F32),
            pltpu.VMEM((nheads, tq, tq), BF16),
            pltpu.VMEM((nheads, tq, tq), BF16),
        ],
        compiler_params=_cparams(("arbitrary", "arbitrary", "arbitrary")),
        name="sb_attn",
    )(main3, main3, main3, gain)


def _ssd_kernel(z_ref, xs_ref, bc_ref, dt_ref, cwx_ref, cbx_ref, cwb_ref, cbb_ref, dtb_ref,
                alog_ref, dexp_ref, ng_ref, e_ref, o_ref,
                extx_ref, extb_ref, state_ref):
    L = SSD_CHUNK
    hpg = SSD_HEADS // SSD_GROUPS
    gw = hpg * HEAD_DIM

    @pl.when(pl.program_id(1) == 0)
    def _():
        extx_ref[0:8, :] = jnp.zeros((8, D_SSD), F32)
        extb_ref[0:8, :] = jnp.zeros((8, 2 * D_BC), F32)
        state_ref[...] = jnp.zeros_like(state_ref)

    def conv_silu(raw_ref, ext_ref, w_ref, b_ref):
        raw = raw_ref[0].astype(F32)
        ext_ref[8:8 + L, :] = raw
        y = (b_ref[...] + w_ref[3:4, :] * raw
             + w_ref[2:3, :] * ext_ref[7:7 + L, :]
             + w_ref[1:2, :] * ext_ref[6:6 + L, :]
             + w_ref[0:1, :] * ext_ref[5:5 + L, :])
        ext_ref[0:8, :] = raw[L - 8:L, :]
        return y * _sigmoid(y)

    xs = conv_silu(xs_ref, extx_ref, cwx_ref, cbx_ref)
    bc = conv_silu(bc_ref, extb_ref, cwb_ref, cbb_ref)

    xdt_raw = dt_ref[0] + dtb_ref[...]
    dt = jnp.maximum(xdt_raw, 0.0) + _softplus_neg_abs(xdt_raw)
    a = -jnp.exp(alog_ref[...])
    da = dt * a
    row = lax.broadcasted_iota(jnp.int32, (L, L), 0)
    col = lax.broadcasted_iota(jnp.int32, (L, L), 1)
    causal = row >= col
    tri = jnp.where(causal, 1.0, 0.0).astype(BF16)
    a_cs = _mask_dot_left(tri, da, 3)
    a_cs_t = a_cs.T
    a_last = a_cs[L - 1:L, :]

    e01 = e_ref[...]
    dt_x = _mask_dot_right(dt, e01, 3)
    dec_in_x = _mask_dot_right(jnp.exp(a_cs), e01, 3)
    dec_end_x = _mask_dot_right(jnp.exp(a_last - a_cs), e01, 3)
    chunk_dec_x = _mask_dot_right(jnp.exp(a_last), e01, 3)

    xdt = xs * dt_x
    xdt_b = xdt.astype(BF16)
    xend_b = (xdt * dec_end_x).astype(BF16)

    lane = lax.broadcasted_iota(jnp.int32, (1, LANES), 1)
    ys = []
    for g in range(SSD_GROUPS):
        b_g = bc[:, g * SSD_STATE:(g + 1) * SSD_STATE].astype(BF16)
        c_g = bc[:, D_BC + g * SSD_STATE:D_BC + (g + 1) * SSD_STATE].astype(BF16)
        cb = _dot_nt(c_g, b_g)
        st = state_ref[g]
        y_off = _dot(c_g, st.astype(BF16)) * dec_in_x[:, g * gw:(g + 1) * gw]
        y_diag = []
        for pr in range(hpg // 2):
            outs = []
            for hh in range(2):
                h = g * hpg + 2 * pr + hh
                seg = jnp.exp(jnp.where(causal, a_cs[:, h:h + 1] - a_cs_t[h:h + 1, :], -jnp.inf))
                m = (cb * seg).astype(BF16)
                lo = (g * hpg + 2 * pr) * HEAD_DIM
                outs.append(_dot(m, xdt_b[:, lo:lo + LANES]))
            y_diag.append(jnp.where(lane < HEAD_DIM, outs[0], outs[1]))
        y_g = jnp.concatenate(y_diag, axis=-1) + y_off
        ys.append(y_g)
        state_ref[g] = (chunk_dec_x[:, g * gw:(g + 1) * gw] * st
                        + _dot_tn(b_g, xend_b[:, g * gw:(g + 1) * gw]))

    zf = z_ref[0].astype(F32)
    gate = zf * _sigmoid(zf)
    for g in range(SSD_GROUPS):
        sl = slice(g * gw, (g + 1) * gw)
        y_g = (ys[g] + xs[:, sl] * dexp_ref[:, sl]) * gate[:, sl]
        o_ref[0, :, sl] = _rms(y_g, ng_ref[:, sl]).astype(BF16)


def _ssd(main3, small3, cw, cb, dtb, alog, dexp, ng, e01):
    b, s, _ = main3.shape
    L = SSD_CHUNK
    zb, xb, bcb = OFF_Z // D_SSD, OFF_XS // D_SSD, OFF_BC // (2 * D_BC)
    cwx, cwb = cw[:, :D_SSD], cw[:, D_SSD:]
    cbx, cbb = cb[:, :D_SSD], cb[:, D_SSD:]
    const = lambda shape: pl.BlockSpec(shape, lambda bi, c: (0,) * len(shape))
    return pl.pallas_call(
        _ssd_kernel,
        grid=(b, s // L),
        in_specs=[
            pl.BlockSpec((1, L, D_SSD), lambda bi, c: (bi, c, zb)),
            pl.BlockSpec((1, L, D_SSD), lambda bi, c: (bi, c, xb)),
            pl.BlockSpec((1, L, 2 * D_BC), lambda bi, c: (bi, c, bcb)),
            pl.BlockSpec((1, L, LANES), lambda bi, c: (bi, c, 0)),
            const((SSD_CONV, D_SSD)), const((1, D_SSD)),
            const((SSD_CONV, 2 * D_BC)), const((1, 2 * D_BC)),
            const((1, LANES)), const((1, LANES)),
            const((1, D_SSD)), const((1, D_SSD)),
            const((LANES, D_SSD)),
        ],
        out_specs=pl.BlockSpec((1, L, D_SSD), lambda bi, c: (bi, c, 0)),
        out_shape=jax.ShapeDtypeStruct((b, s, D_SSD), BF16),
        scratch_shapes=[
            pltpu.VMEM((L + 8, D_SSD), F32),
            pltpu.VMEM((L + 8, 2 * D_BC), F32),
            pltpu.VMEM((SSD_GROUPS, SSD_STATE, D_SSD // SSD_GROUPS), F32),
        ],
        compiler_params=_cparams(("arbitrary", "arbitrary")),
        name="ssd",
    )(main3, main3, main3, small3, cwx, cbx, cwb, cbb, dtb, alog, dexp, ng, e01)


def _ffn_kernel(x_ref, yf_ref, ys_ref, yd_ref, wof_ref, wos_ref, wod_ref, g_ref, wu_ref, cw_ref, cb_ref,
                wd_ref, fg_ref, o_ref, carry_ref, u_ref, act_ref, *, tm, tiles_per_seq, final_norm):
    fc = FFN_FC
    cw2 = 2 * fc
    nchunk = D_FF // fc

    @pl.when(pl.program_id(0) % tiles_per_seq == 0)
    def _():
        carry_ref[...] = jnp.zeros_like(carry_ref)

    x = (x_ref[...] + _dot(yf_ref[...], wof_ref[...]) + _dot(ys_ref[...], wos_ref[...])
         + _dot(yd_ref[...], wod_ref[...]))
    h = _rms(x, g_ref[...]).astype(BF16)

    def up(c):
        cs = slice(c * cw2, (c + 1) * cw2)
        u_ref[c % 2, 0:8, :] = carry_ref[:, cs]
        u_ref[c % 2, 8:8 + tm, :] = _dot(h, wu_ref[:, cs])
        carry_ref[:, cs] = u_ref[c % 2, tm:tm + 8, :]

    up(0)
    for c in range(nchunk):
        if c + 1 < nchunk:
            up(c + 1)
        cs = slice(c * cw2, (c + 1) * cw2)
        y = cb_ref[:, cs] + cw_ref[FFN_CONV - 1:FFN_CONV, cs] * u_ref[c % 2, 8:8 + tm, :]
        for k in range(1, FFN_CONV):
            y = y + cw_ref[FFN_CONV - 1 - k:FFN_CONV - k, cs] * u_ref[c % 2, 8 - k:8 - k + tm, :]
        gate = y[:, :fc]
        act_ref[:, c * fc:(c + 1) * fc] = (gate * _sigmoid(gate) * y[:, fc:]).astype(BF16)
    out = x + _dot(act_ref[...], wd_ref[...])
    if final_norm:
        out = _rms(out, fg_ref[...])
    o_ref[...] = out


def _ffn(x2, yf, ys, yd, wof, wos, wod, g, wu, cw, cb, wd, fg, seq, final_norm):
    t = x2.shape[0]
    tm = TM_PROJ
    tok = lambda n: pl.BlockSpec((tm, n), lambda i: (i, 0))
    row = lambda n: pl.BlockSpec((1, n), lambda i: (0, 0))
    resident = lambda r, n: pl.BlockSpec((r, n), lambda i: (0, 0), pipeline_mode=pl.Buffered(1))
    return pl.pallas_call(
        functools.partial(_ffn_kernel, tm=tm, tiles_per_seq=seq // tm, final_norm=final_norm),
        grid=(t // tm,),
        in_specs=[
            tok(D_MODEL), tok(D_FOX), tok(D_SB), tok(D_SSD),
            resident(D_FOX, D_MODEL), resident(D_SB, D_MODEL), resident(D_SSD, D_MODEL),
            row(D_MODEL),
            resident(D_MODEL, 2 * D_FF),
            pl.BlockSpec((FFN_CONV, 2 * D_FF), lambda i: (0, 0)),
            row(2 * D_FF),
            resident(D_FF, D_MODEL),
            row(D_MODEL),
        ],
        out_specs=tok(D_MODEL),
        out_shape=jax.ShapeDtypeStruct((t, D_MODEL), F32),
        scratch_shapes=[
            pltpu.VMEM((8, 2 * D_FF), F32),
            pltpu.VMEM((2, tm + 8, 2 * FFN_FC), F32),
            pltpu.VMEM((tm, D_FF), BF16),
        ],
        compiler_params=_cparams(("arbitrary",)),
        name="ffn",
    )(x2, yf, ys, yd, wof, wos, wod, g, wu, cw, cb, wd, fg)


def _interleave_ff(a):
    lead = a.shape[:-1]
    n = D_FF // FFN_FC
    a = a.reshape(lead + (2, n, FFN_FC))
    a = jnp.swapaxes(a, -3, -2)
    return a.reshape(lead + (2 * D_FF,))


def _pad_lanes(v, offset):
    return jnp.zeros((1, LANES), F32).at[0, offset:offset + v.shape[0]].set(v.astype(F32))


def _layer(x2, b, s, mix_g, w_in, fox_f_bias, fox_out_g, sb_out_g, ssd_conv_w, ssd_conv_b,
           ssd_dt_bias, ssd_a_log, ssd_d, ssd_norm_g, w_out, ffn_g, w_up, ffn_conv_w, ffn_conv_b,
           w_down, final_g, final_norm):
    o = 0
    cols = {}
    for name, n in (("fq", D_FOX), ("fk", D_FOX), ("fv", D_FOX), ("ff", FOX_HEADS),
                    ("sq", D_SB), ("sk", D_SB), ("sv", D_SB),
                    ("z", D_SSD), ("xbc", D_SSD + 2 * D_BC), ("dt", SSD_HEADS)):
        cols[name] = w_in[:, o:o + n]
        o += n
    wm = jnp.concatenate([cols[k] for k in ("fq", "fk", "fv", "sq", "sk", "sv", "z", "xbc")],
                         axis=1).astype(BF16)
    ws = jnp.concatenate([cols["dt"], cols["ff"],
                          jnp.zeros((D_MODEL, LANES - SSD_HEADS - FOX_HEADS), F32)], axis=1).astype(BF16)

    main2, small2 = _inproj(x2, mix_g[None, :], wm, ws)
    main3 = main2.reshape(b, s, N_MAIN)
    small3 = small2.reshape(b, s, LANES)

    c_full, cp = _fox_gate(small3, _pad_lanes(fox_f_bias, SMALL_FF))
    c8 = c_full[:, :, SMALL_FF:SMALL_FF + FOX_HEADS]
    hps = 2 * FOX_PAIRS_PER_STEP
    cq = c8.reshape(b, s, FOX_HEADS // hps, hps).transpose(0, 2, 1, 3)
    y_fox = _fox_attention(main3, cp, cq, fox_out_g[None, :])
    y_sb = _sb_attention(main3, sb_out_g[None, :])

    e01 = (jnp.arange(LANES)[:, None] == (jnp.arange(D_SSD)[None, :] // HEAD_DIM)).astype(BF16)
    y_ssd = _ssd(main3, small3, ssd_conv_w, ssd_conv_b[None, :],
                 _pad_lanes(ssd_dt_bias, SMALL_DT), _pad_lanes(ssd_a_log, SMALL_DT),
                 jnp.repeat(ssd_d, HEAD_DIM)[None, :], ssd_norm_g[None, :], e01)

    t = b * s
    wo = w_out.astype(BF16)
    return _ffn(x2, y_fox.reshape(t, D_FOX), y_sb.reshape(t, D_SB), y_ssd.reshape(t, D_SSD),
                wo[:D_FOX], wo[D_FOX:D_FOX + D_SB], wo[D_FOX + D_SB:],
                ffn_g[None, :], _interleave_ff(w_up.astype(BF16)), _interleave_ff(ffn_conv_w),
                _interleave_ff(ffn_conv_b)[None, :], w_down.astype(BF16), final_g[None, :], s,
                final_norm)


def kernel(x, mix_norm_g, w_in, fox_f_bias, fox_out_g, sb_out_g, ssd_conv_w, ssd_conv_b, ssd_dt_bias,
           ssd_a_log, ssd_d, ssd_norm_g, w_out, ffn_norm_g, w_up, ffn_conv_w, ffn_conv_b, w_down,
           final_norm_g):
    b, s, d = x.shape
    depth = w_in.shape[0]
    x2 = x.reshape(b * s, d)
    for l in range(depth):
        x2 = _layer(x2, b, s, mix_norm_g[l], w_in[l], fox_f_bias[l], fox_out_g[l], sb_out_g[l],
                    ssd_conv_w[l], ssd_conv_b[l], ssd_dt_bias[l], ssd_a_log[l], ssd_d[l],
                    ssd_norm_g[l], w_out[l], ffn_norm_g[l], w_up[l], ffn_conv_w[l], ffn_conv_b[l],
                    w_down[l], final_norm_g, l == depth - 1)
    return x2.reshape(b, s, d)
```

```python
import functools

import jax
import jax.numpy as jnp
from jax import lax
from jax.experimental import pallas as pl
from jax.experimental.pallas import tpu as pltpu

F32 = jnp.float32
BF16 = jnp.bfloat16

D_MODEL = 1024
HEAD_DIM = 64
FOX_HEADS = 8
SB_HEADS = 8
SSD_HEADS = 16
SSD_GROUPS = 2
SSD_STATE = 128
SSD_CONV = 4
SSD_CHUNK = 128
D_FOX = FOX_HEADS * HEAD_DIM
D_SB = SB_HEADS * HEAD_DIM
D_SSD = SSD_HEADS * HEAD_DIM
D_BC = SSD_GROUPS * SSD_STATE
D_FF = 2816
FFN_CONV = 3
NORM_EPS = 1e-6
CP_TERMS = 3
EXP_UNDERFLOW = -104.0

LANES = 128
N_MAIN = 3 * D_FOX + 3 * D_SB + D_SSD + D_SSD + 2 * D_BC
OFF_FOX = 0
OFF_SB = 3 * D_FOX
OFF_Z = OFF_SB + 3 * D_SB
OFF_XS = OFF_Z + D_SSD
OFF_BC = OFF_XS + D_SSD
SMALL_DT = 0
SMALL_FF = SSD_HEADS

VMEM_LIMIT = 56 * 1024 * 1024

TM_PROJ = 512
TQ = 256
TQ_FOX = 512
SSD_CHUNKS_PER_STEP = 4
EXPAND_PAD = 16
ROW_BLOCK = 32
FOX_PAIRS_PER_STEP = 2
SB_PAIRS_PER_STEP = 4
FFN_FC = 256


def _cparams(sem):
    return pltpu.CompilerParams(dimension_semantics=sem, vmem_limit_bytes=VMEM_LIMIT)


def _split_bf16(x, n):
    parts, r = [], x
    for _ in range(n):
        p = r.astype(BF16)
        parts.append(p)
        r = r - p.astype(F32)
    return parts


def _dot(a, b):
    return jnp.dot(a, b, preferred_element_type=F32)


def _dot_nt(a, b):
    return lax.dot_general(a, b, (((1,), (1,)), ((), ())), preferred_element_type=F32)


def _dot_tn(a, b):
    return lax.dot_general(a, b, (((0,), (0,)), ((), ())), preferred_element_type=F32)


def _mask_dot_left(m01, x, n):
    out = None
    for p in _split_bf16(x, n):
        t = _dot(m01, p)
        out = t if out is None else out + t
    return out


def _softplus_neg_abs(x):
    return jnp.log(1.0 + jnp.exp(-jnp.abs(x)))


def _sigmoid(x):
    return 1.0 / (1.0 + jnp.exp(-x))


def _rms(x, g):
    ms = jnp.mean(x * x, axis=-1, keepdims=True)
    return x * lax.rsqrt(ms + NORM_EPS) * g


def _head_pair_norm(o2, g, lane):
    lo = lane < HEAD_DIM
    sq = o2 * o2
    ms0 = jnp.sum(jnp.where(lo, sq, 0.0), axis=-1, keepdims=True) * (1.0 / HEAD_DIM)
    ms1 = jnp.sum(jnp.where(lo, 0.0, sq), axis=-1, keepdims=True) * (1.0 / HEAD_DIM)
    ms = jnp.where(lo, ms0, ms1)
    return o2 * lax.rsqrt(ms + NORM_EPS) * g


def _inproj_kernel(x_ref, g_ref, wm_ref, ws_ref, cw_ref, cb_ref, main_ref, small_ref, carry_ref, u_ref,
                   *, tm, nc, tiles_per_seq):
    @pl.when(pl.program_id(0) % tiles_per_seq == 0)
    def _():
        carry_ref[...] = jnp.zeros_like(carry_ref)

    h = _rms(x_ref[...], g_ref[...]).astype(BF16)
    order = list(reversed(range(0, N_MAIN, nc)))

    def project(n):
        c = order[n]
        u_ref[n % 2, 8:8 + tm, :] = _dot(h, wm_ref[:, c:c + nc])
        if c >= OFF_XS:
            cc = slice(c - OFF_XS, c - OFF_XS + nc)
            u_ref[n % 2, 0:8, :] = carry_ref[:, cc]
            carry_ref[:, cc] = u_ref[n % 2, tm:tm + 8, :]

    project(0)
    for n, c in enumerate(order):
        if n + 1 < len(order):
            project(n + 1)
        acc = u_ref[n % 2, 8:8 + tm, :]
        if c >= OFF_XS:
            cc = slice(c - OFF_XS, c - OFF_XS + nc)
            acc = cb_ref[:, cc] + cw_ref[SSD_CONV - 1:SSD_CONV, cc] * acc
            for k in range(1, SSD_CONV):
                acc = acc + cw_ref[SSD_CONV - 1 - k:SSD_CONV - k, cc] * u_ref[n % 2, 8 - k:8 - k + tm, :]
        if c >= OFF_Z:
            acc = acc * _sigmoid(acc)
        main_ref[:, c:c + nc] = acc.astype(BF16)
    small_ref[...] = _dot(h, ws_ref[...])


def _inproj(x2, g, wm, ws, cw, cb, seq):
    t = x2.shape[0]
    tm = TM_PROJ
    nc = 512
    n_conv = N_MAIN - OFF_XS
    return pl.pallas_call(
        functools.partial(_inproj_kernel, tm=tm, nc=nc, tiles_per_seq=seq // tm),
        grid=(t // tm,),
        in_specs=[
            pl.BlockSpec((tm, D_MODEL), lambda i: (i, 0)),
            pl.BlockSpec((1, D_MODEL), lambda i: (0, 0)),
            pl.BlockSpec((D_MODEL, N_MAIN), lambda i: (0, 0), pipeline_mode=pl.Buffered(1)),
            pl.BlockSpec((D_MODEL, LANES), lambda i: (0, 0), pipeline_mode=pl.Buffered(1)),
            pl.BlockSpec((SSD_CONV, n_conv), lambda i: (0, 0)),
            pl.BlockSpec((1, n_conv), lambda i: (0, 0)),
        ],
        out_specs=[
            pl.BlockSpec((tm, N_MAIN), lambda i: (i, 0)),
            pl.BlockSpec((tm, LANES), lambda i: (i, 0)),
        ],
        out_shape=[
            jax.ShapeDtypeStruct((t, N_MAIN), BF16),
            jax.ShapeDtypeStruct((t, LANES), F32),
        ],
        scratch_shapes=[
            pltpu.VMEM((8, n_conv), F32),
            pltpu.VMEM((2, tm + 8, nc), F32),
        ],
        compiler_params=_cparams(("arbitrary",)),
        name="inproj",
    )(x2, g, wm, ws, cw, cb)


def _fox_gate_kernel(s_ref, bias_ref, c_ref, cp_ref, carry_ref, *, tb):
    @pl.when(pl.program_id(1) == 0)
    def _():
        carry_ref[...] = jnp.zeros_like(carry_ref)

    xx = s_ref[0] + bias_ref[...]
    log_f = jnp.minimum(xx, 0.0) - _softplus_neg_abs(xx)
    row = lax.broadcasted_iota(jnp.int32, (tb, tb), 0)
    col = lax.broadcasted_iota(jnp.int32, (tb, tb), 1)
    tri = jnp.where(row >= col, 1.0, 0.0).astype(BF16)
    cum = _mask_dot_left(tri, log_f, 3) + carry_ref[...]
    c_ref[0] = cum
    carry_ref[...] = cum[tb - 1:tb, :]
    src = lax.broadcasted_iota(jnp.int32, (LANES, LANES), 0) - SMALL_FF
    dst = lax.broadcasted_iota(jnp.int32, (LANES, LANES), 1)
    head_ok = (src >= 0) & (src < FOX_HEADS)
    cp = None
    for j, part in enumerate(_split_bf16(-cum, CP_TERMS)):
        sel = jnp.where(head_ok & (dst == CP_TERMS * src + j), 1.0, 0.0).astype(BF16)
        t = _dot(part, sel)
        cp = t if cp is None else cp + t
    cp_ref[0] = cp.astype(BF16)


def _fox_gate(small3, bias_row):
    b, s, _ = small3.shape
    tb = 512
    blk = pl.BlockSpec((1, tb, LANES), lambda bi, i: (bi, i, 0))
    return pl.pallas_call(
        functools.partial(_fox_gate_kernel, tb=tb),
        grid=(b, s // tb),
        in_specs=[blk, pl.BlockSpec((1, LANES), lambda bi, i: (0, 0))],
        out_specs=[blk, blk],
        out_shape=[jax.ShapeDtypeStruct((b, s, LANES), F32),
                   jax.ShapeDtypeStruct((b, s, LANES), BF16)],
        scratch_shapes=[pltpu.VMEM((1, LANES), F32)],
        compiler_params=_cparams(("arbitrary", "arbitrary")),
        name="fox_gate",
    )(small3, bias_row)


def _fox_kernel(q_ref, k_ref, v_ref, cp_ref, cq_ref, g_ref, o_ref, vh_ref, m_ref, acc_ref, c_ref,
                alpha_ref, t_ref, p_ref, *, tq, npairs):
    p_idx = pl.program_id(1)
    i = pl.program_id(2)
    lane = lax.broadcasted_iota(jnp.int32, (1, LANES), 1)
    lo_half = lane < HEAD_DIM
    nheads = 2 * npairs
    pair = lambda h: slice((h // 2) * LANES, (h // 2 + 1) * LANES)

    @pl.when(i == 0)
    def _():
        for pr in range(npairs):
            v2 = v_ref[0, :, pair(2 * pr)]
            one = jnp.ones_like(v2)
            vh_ref[2 * pr] = jnp.where(lo_half, v2, one)
            vh_ref[2 * pr + 1] = jnp.where(lo_half, one, v2)

    q_aug = []
    for h in range(nheads):
        q2 = q_ref[0, :, pair(h)]
        own = lo_half if h % 2 == 0 else jnp.logical_not(lo_half)
        first = CP_TERMS * (nheads * p_idx + h)
        ones_at = jnp.where((lane >= first) & (lane < first + CP_TERMS), 1.0, 0.0).astype(BF16)
        q_aug.append(jnp.concatenate(
            [jnp.where(own, q2, jnp.zeros_like(q2)) * 0.125, jnp.broadcast_to(ones_at, (tq, LANES))],
            axis=1))
        c_ref[h] = jnp.broadcast_to(cq_ref[0, 0, :, h:h + 1], (tq, LANES))

    m_ref[...] = jnp.full_like(m_ref, -jnp.inf)
    acc_ref[...] = jnp.zeros_like(acc_ref)

    def tile(start, tk, masked):
        cp_t = cp_ref[0, pl.ds(start, tk), :]
        for h in range(nheads):
            k_aug = jnp.concatenate([k_ref[0, pl.ds(start, tk), pair(h)], cp_t], axis=1)
            t_ref[h, :, 0:tk] = _dot_nt(q_aug[h], k_aug)
        for h in range(nheads):
            for r in range(0, tq, ROW_BLOCK):
                rows = slice(r, r + ROW_BLOCK)
                t = t_ref[h, rows, 0:tk]
                if masked:
                    row = lax.broadcasted_iota(jnp.int32, (ROW_BLOCK, tk), 0) + r
                    col = lax.broadcasted_iota(jnp.int32, (ROW_BLOCK, tk), 1)
                    t = jnp.where(row >= col, t, -jnp.inf)
                m_prev = m_ref[h, rows, :]
                m_new = jnp.maximum(m_prev, jnp.max(t, axis=-1, keepdims=True) + c_ref[h, rows, :])
                shift = m_new - c_ref[h, rows, :]
                for c in range(0, tk, LANES):
                    p_ref[h, rows, c:c + LANES] = jnp.exp(t[:, c:c + LANES] - shift).astype(BF16)
                alpha_ref[h, rows, :] = jnp.exp(m_prev - m_new)
                m_ref[h, rows, :] = m_new
        for h in range(nheads):
            acc_ref[h] = alpha_ref[h] * acc_ref[h] + _dot(p_ref[h, :, 0:tk], vh_ref[h, pl.ds(start, tk), :])

    def body(j, carry):
        tile(pl.multiple_of(j * (2 * tq), 2 * tq), 2 * tq, False)
        return carry

    lax.fori_loop(0, i // 2, body, 0)

    @pl.when(i % 2 == 1)
    def _():
        tile(pl.multiple_of((i - 1) * tq, tq), tq, False)

    tile(pl.multiple_of(i * tq, tq), tq, True)

    for pr in range(npairs):
        a0, a1 = acc_ref[2 * pr], acc_ref[2 * pr + 1]
        num = jnp.where(lo_half, a0, a1)
        den = pltpu.roll(jnp.where(lo_half, a1, a0), HEAD_DIM, axis=1)
        o_ref[0, :, pair(2 * pr)] = _head_pair_norm(num / den, g_ref[:, pair(2 * pr)], lane).astype(BF16)


def _fox_attention(main3, cp, cq, gain):
    b, s, _ = main3.shape
    tq = TQ_FOX
    npairs = FOX_PAIRS_PER_STEP
    nheads = 2 * npairs
    w = npairs * LANES
    nsteps = D_FOX // w
    qb, kb, vb = OFF_FOX // w, (OFF_FOX + D_FOX) // w, (OFF_FOX + 2 * D_FOX) // w
    return pl.pallas_call(
        functools.partial(_fox_kernel, tq=tq, npairs=npairs),
        grid=(b, nsteps, s // tq),
        in_specs=[
            pl.BlockSpec((1, tq, w), lambda bi, p, i: (bi, i, qb + p)),
            pl.BlockSpec((1, s, w), lambda bi, p, i: (bi, 0, kb + p)),
            pl.BlockSpec((1, s, w), lambda bi, p, i: (bi, 0, vb + p)),
            pl.BlockSpec((1, s, LANES), lambda bi, p, i: (bi, 0, 0)),
            pl.BlockSpec((1, 1, tq, nheads), lambda bi, p, i: (bi, p, i, 0)),
            pl.BlockSpec((1, w), lambda bi, p, i: (0, p)),
        ],
        out_specs=pl.BlockSpec((1, tq, w), lambda bi, p, i: (bi, i, p)),
        out_shape=jax.ShapeDtypeStruct((b, s, D_FOX), BF16),
        scratch_shapes=[
            pltpu.VMEM((nheads, s, LANES), BF16),
            pltpu.VMEM((nheads, tq, LANES), F32),
            pltpu.VMEM((nheads, tq, LANES), F32),
            pltpu.VMEM((nheads, tq, LANES), F32),
            pltpu.VMEM((nheads, tq, LANES), F32),
            pltpu.VMEM((nheads, tq, 2 * tq), F32),
            pltpu.VMEM((nheads, tq, 2 * tq), BF16),
        ],
        compiler_params=_cparams(("arbitrary", "arbitrary", "arbitrary")),
        name="fox_attn",
    )(main3, main3, main3, cp, cq, gain)


def _sb_kernel(q_ref, k_ref, v_ref, g_ref, o_ref, r_ref, acc_ref, z_ref, lb_ref, lk_ref, w_ref, *,
               tq, npairs):
    i = pl.program_id(2)
    lane = lax.broadcasted_iota(jnp.int32, (1, LANES), 1)
    nheads = 2 * npairs
    qh = []
    for pr in range(npairs):
        q2 = q_ref[0, :, pr * LANES:(pr + 1) * LANES]
        zero = jnp.zeros_like(q2)
        qh += [jnp.where(lane < HEAD_DIM, q2, zero) * 0.125,
               jnp.where(lane < HEAD_DIM, zero, q2) * 0.125]
    row = lax.broadcasted_iota(jnp.int32, (tq, tq), 0)
    col = lax.broadcasted_iota(jnp.int32, (tq, tq), 1)
    later = jnp.where(row > col, 1.0, 0.0).astype(BF16)

    r_ref[...] = jnp.zeros_like(r_ref)
    acc_ref[...] = jnp.zeros_like(acc_ref)

    def tile(j, masked):
        start = pl.multiple_of(j * tq, tq)
        pair = lambda hh: slice((hh // 2) * LANES, (hh // 2 + 1) * LANES)
        for hh in range(nheads):
            z_ref[hh] = _dot_nt(qh[hh], k_ref[0, pl.ds(start, tq), pair(hh)])
        r_prev = []
        for hh in range(nheads):
            z = z_ref[hh]
            log_beta = jnp.minimum(z, 0.0) - _softplus_neg_abs(z)
            log_keep = log_beta - z
            if masked:
                log_keep = jnp.where(col < row, log_keep, 0.0)
            lb_ref[hh] = log_beta
            lk_ref[hh] = log_keep.astype(BF16)
            r_prev.append(r_ref[hh])
            r_ref[hh] = r_prev[hh] + jnp.sum(log_keep, axis=-1, keepdims=True)
        for hh in range(nheads):
            z_ref[hh] = _dot(lk_ref[hh], later)
        for hh in range(nheads):
            log_w = lb_ref[hh] + z_ref[hh]
            chunks = []
            for c in range(0, tq, LANES):
                lw = log_w[:, c:c + LANES] + r_prev[hh]
                if masked:
                    row_c = lax.broadcasted_iota(jnp.int32, (tq, LANES), 0)
                    col_c = lax.broadcasted_iota(jnp.int32, (tq, LANES), 1) + c
                    lw = jnp.where(col_c < row_c, lw, -jnp.inf)
                chunks.append(jnp.exp(lw).astype(BF16))
            w_ref[hh] = jnp.concatenate(chunks, axis=1)
        for hh in range(nheads):
            acc_ref[hh] = acc_ref[hh] + _dot(w_ref[hh], v_ref[0, pl.ds(start, tq), pair(hh)])

    tile(i, True)

    def cond(carry):
        it, live = carry
        return jnp.logical_and(it < i, live)

    def body(carry):
        it, _ = carry
        tile(i - 1 - it, False)
        return it + 1, jnp.max(r_ref[...]) > EXP_UNDERFLOW

    lax.while_loop(cond, body, (0, True))

    for pr in range(npairs):
        ps = slice(pr * LANES, (pr + 1) * LANES)
        o2 = jnp.where(lane < HEAD_DIM, acc_ref[2 * pr], acc_ref[2 * pr + 1])
        o_ref[0, :, ps] = _head_pair_norm(o2, g_ref[:, ps], lane).astype(BF16)


def _sb_attention(main3, gain):
    b, s, _ = main3.shape
    tq = TQ
    npairs = SB_PAIRS_PER_STEP
    nheads = 2 * npairs
    w = npairs * LANES
    nsteps = D_SB // w
    qb, kb, vb = OFF_SB // w, (OFF_SB + D_SB) // w, (OFF_SB + 2 * D_SB) // w
    return pl.pallas_call(
        functools.partial(_sb_kernel, tq=tq, npairs=npairs),
        grid=(b, nsteps, s // tq),
        in_specs=[
            pl.BlockSpec((1, tq, w), lambda bi, p, i: (bi, i, qb + p)),
            pl.BlockSpec((1, s, w), lambda bi, p, i: (bi, 0, kb + p)),
            pl.BlockSpec((1, s, w), lambda bi, p, i: (bi, 0, vb + p)),
            pl.BlockSpec((1, w), lambda bi, p, i: (0, p)),
        ],
        out_specs=pl.BlockSpec((1, tq, w), lambda bi, p, i: (bi, i, p)),
        out_shape=jax.ShapeDtypeStruct((b, s, D_SB), BF16),
        scratch_shapes=[
            pltpu.VMEM((nheads, tq, LANES), F32),
            pltpu.VMEM((nheads, tq, LANES), F32),
            pltpu.VMEM((nheads, tq, tq), F32),
            pltpu.VMEM((nheads, tq, tq), F32),
            pltpu.VMEM((nheads, tq, tq), BF16),
            pltpu.VMEM((nheads, tq, tq), BF16),
        ],
        compiler_params=_cparams(("arbitrary", "arbitrary", "arbitrary")),
        name="sb_attn",
    )(main3, main3, main3, gain)


def _ssd_kernel(z_ref, xs_ref, bc_ref, dt_ref, dtb_ref, alog_ref, dexp_ref, ng_ref, e_ref, o_ref, state_ref):
    L = SSD_CHUNK
    hpg = SSD_HEADS // SSD_GROUPS
    gw = hpg * HEAD_DIM

    @pl.when(pl.program_id(1) == 0)
    def _():
        state_ref[...] = jnp.zeros_like(state_ref)

    def chunk(rows):
        xs = xs_ref[0, rows, :].astype(F32)
        bc = bc_ref[0, rows, :]

        xdt_raw = dt_ref[0, rows, :] + dtb_ref[...]
        dt = jnp.maximum(xdt_raw, 0.0) + _softplus_neg_abs(xdt_raw)
        a = -jnp.exp(alog_ref[...])
        da = dt * a
        row = lax.broadcasted_iota(jnp.int32, (L, L), 0)
        col = lax.broadcasted_iota(jnp.int32, (L, L), 1)
        causal = row >= col
        tri = jnp.where(causal, 1.0, 0.0).astype(BF16)
        a_cs = _mask_dot_left(tri, da, 3)
        a_cs_t = a_cs.T
        a_last = a_cs[L - 1:L, :]

        per_head = jnp.concatenate([dt, jnp.exp(a_cs), jnp.exp(a_last - a_cs),
                                    jnp.broadcast_to(jnp.exp(a_last), (EXPAND_PAD, LANES))], axis=0)
        wide = _dot(jnp.concatenate(_split_bf16(per_head, 2), axis=1), e_ref[...])
        dt_x = wide[0:L]
        dec_in_x = wide[L:2 * L]
        dec_end_x = wide[2 * L:3 * L]
        chunk_dec_x = wide[3 * L:3 * L + 1]

        xdt = xs * dt_x
        xdt_b = xdt.astype(BF16)
        xend_b = (xdt * dec_end_x).astype(BF16)

        lane = lax.broadcasted_iota(jnp.int32, (1, LANES), 1)
        ys = []
        for g in range(SSD_GROUPS):
            b_g = bc[:, g * SSD_STATE:(g + 1) * SSD_STATE]
            c_g = bc[:, D_BC + g * SSD_STATE:D_BC + (g + 1) * SSD_STATE]
            cb = _dot_nt(c_g, b_g)
            st = state_ref[g]
            y_off = _dot(c_g, st.astype(BF16)) * dec_in_x[:, g * gw:(g + 1) * gw]
            y_diag = []
            for pr in range(hpg // 2):
                outs = []
                for hh in range(2):
                    h = g * hpg + 2 * pr + hh
                    seg = jnp.exp(jnp.where(causal, a_cs[:, h:h + 1] - a_cs_t[h:h + 1, :], -jnp.inf))
                    m = (cb * seg).astype(BF16)
                    lo = (g * hpg + 2 * pr) * HEAD_DIM
                    outs.append(_dot(m, xdt_b[:, lo:lo + LANES]))
                y_diag.append(jnp.where(lane < HEAD_DIM, outs[0], outs[1]))
            y_g = jnp.concatenate(y_diag, axis=-1) + y_off
            ys.append(y_g)
            state_ref[g] = (chunk_dec_x[:, g * gw:(g + 1) * gw] * st
                            + _dot_tn(b_g, xend_b[:, g * gw:(g + 1) * gw]))

        gate = z_ref[0, rows, :].astype(F32)
        for g in range(SSD_GROUPS):
            sl = slice(g * gw, (g + 1) * gw)
            y_g = (ys[g] + xs[:, sl] * dexp_ref[:, sl]) * gate[:, sl]
            o_ref[0, rows, sl] = _rms(y_g, ng_ref[:, sl]).astype(BF16)

    for sub in range(SSD_CHUNKS_PER_STEP):
        chunk(slice(sub * L, (sub + 1) * L))


def _ssd(main3, small3, dtb, alog, dexp, ng, e01):
    b, s, _ = main3.shape
    L = SSD_CHUNK * SSD_CHUNKS_PER_STEP
    zb, xb, bcb = OFF_Z // D_SSD, OFF_XS // D_SSD, OFF_BC // (2 * D_BC)
    const = lambda shape: pl.BlockSpec(shape, lambda bi, c: (0,) * len(shape))
    return pl.pallas_call(
        _ssd_kernel,
        grid=(b, s // L),
        in_specs=[
            pl.BlockSpec((1, L, D_SSD), lambda bi, c: (bi, c, zb)),
            pl.BlockSpec((1, L, D_SSD), lambda bi, c: (bi, c, xb)),
            pl.BlockSpec((1, L, 2 * D_BC), lambda bi, c: (bi, c, bcb)),
            pl.BlockSpec((1, L, LANES), lambda bi, c: (bi, c, 0)),
            const((1, LANES)), const((1, LANES)),
            const((1, D_SSD)), const((1, D_SSD)),
            const((2 * LANES, D_SSD)),
        ],
        out_specs=pl.BlockSpec((1, L, D_SSD), lambda bi, c: (bi, c, 0)),
        out_shape=jax.ShapeDtypeStruct((b, s, D_SSD), BF16),
        scratch_shapes=[pltpu.VMEM((SSD_GROUPS, SSD_STATE, D_SSD // SSD_GROUPS), F32)],
        compiler_params=_cparams(("arbitrary", "arbitrary")),
        name="ssd",
    )(main3, main3, main3, small3, dtb, alog, dexp, ng, e01)


def _ffn_kernel(x_ref, yf_ref, ys_ref, yd_ref, wof_ref, wos_ref, wod_ref, g_ref, wu_ref, cw_ref, cb_ref,
                wd_ref, fg_ref, o_ref, carry_ref, u_ref, act_ref, *, tm, tiles_per_seq, final_norm):
    fc = FFN_FC
    nchunk = D_FF // fc

    @pl.when(pl.program_id(0) % tiles_per_seq == 0)
    def _():
        carry_ref[...] = jnp.zeros_like(carry_ref)

    x = (x_ref[...] + _dot(yf_ref[...], wof_ref[...]) + _dot(ys_ref[...], wos_ref[...])
         + _dot(yd_ref[...], wod_ref[...]))
    h = _rms(x, g_ref[...]).astype(BF16)

    def cols(c):
        return slice(c * fc, (c + 1) * fc), slice(D_FF + c * fc, D_FF + (c + 1) * fc)

    def up(c):
        for half, cs in enumerate(cols(c)):
            hs = slice(half * fc, (half + 1) * fc)
            u_ref[c % 2, 0:8, hs] = carry_ref[:, cs]
            u_ref[c % 2, 8:8 + tm, hs] = _dot(h, wu_ref[:, cs])
            carry_ref[:, cs] = u_ref[c % 2, tm:tm + 8, hs]

    up(0)
    for c in range(nchunk):
        if c + 1 < nchunk:
            up(c + 1)
        ys = []
        for half, cs in enumerate(cols(c)):
            hs = slice(half * fc, (half + 1) * fc)
            y = cb_ref[:, cs] + cw_ref[FFN_CONV - 1:FFN_CONV, cs] * u_ref[c % 2, 8:8 + tm, hs]
            for k in range(1, FFN_CONV):
                y = y + cw_ref[FFN_CONV - 1 - k:FFN_CONV - k, cs] * u_ref[c % 2, 8 - k:8 - k + tm, hs]
            ys.append(y)
        gate, val = ys
        act_ref[:, c * fc:(c + 1) * fc] = (gate * _sigmoid(gate) * val).astype(BF16)
    out = x + _dot(act_ref[...], wd_ref[...])
    if final_norm:
        out = _rms(out, fg_ref[...])
    o_ref[...] = out


def _ffn(x2, yf, ys, yd, wof, wos, wod, g, wu, cw, cb, wd, fg, seq, final_norm):
    t = x2.shape[0]
    tm = TM_PROJ
    tok = lambda n: pl.BlockSpec((tm, n), lambda i: (i, 0))
    row = lambda n: pl.BlockSpec((1, n), lambda i: (0, 0))
    resident = lambda r, n: pl.BlockSpec((r, n), lambda i: (0, 0), pipeline_mode=pl.Buffered(1))
    return pl.pallas_call(
        functools.partial(_ffn_kernel, tm=tm, tiles_per_seq=seq // tm, final_norm=final_norm),
        grid=(t // tm,),
        in_specs=[
            tok(D_MODEL), tok(D_FOX), tok(D_SB), tok(D_SSD),
            resident(D_FOX, D_MODEL), resident(D_SB, D_MODEL), resident(D_SSD, D_MODEL),
            row(D_MODEL),
            resident(D_MODEL, 2 * D_FF),
            pl.BlockSpec((FFN_CONV, 2 * D_FF), lambda i: (0, 0)),
            row(2 * D_FF),
            resident(D_FF, D_MODEL),
            row(D_MODEL),
        ],
        out_specs=tok(D_MODEL),
        out_shape=jax.ShapeDtypeStruct((t, D_MODEL), F32),
        scratch_shapes=[
            pltpu.VMEM((8, 2 * D_FF), F32),
            pltpu.VMEM((2, tm + 8, 2 * FFN_FC), F32),
            pltpu.VMEM((tm, D_FF), BF16),
        ],
        compiler_params=_cparams(("arbitrary",)),
        name="ffn",
    )(x2, yf, ys, yd, wof, wos, wod, g, wu, cw, cb, wd, fg)


def _pad_lanes(v, offset):
    return jnp.zeros((1, LANES), F32).at[0, offset:offset + v.shape[0]].set(v.astype(F32))


def _layer(x2, b, s, mix_g, w_in, fox_f_bias, fox_out_g, sb_out_g, ssd_conv_w, ssd_conv_b,
           ssd_dt_bias, ssd_a_log, ssd_d, ssd_norm_g, w_out, ffn_g, w_up, ffn_conv_w, ffn_conv_b,
           w_down, final_g, final_norm):
    o = 0
    cols = {}
    for name, n in (("fq", D_FOX), ("fk", D_FOX), ("fv", D_FOX), ("ff", FOX_HEADS),
                    ("sq", D_SB), ("sk", D_SB), ("sv", D_SB),
                    ("z", D_SSD), ("xbc", D_SSD + 2 * D_BC), ("dt", SSD_HEADS)):
        cols[name] = w_in[:, o:o + n]
        o += n
    wm = jnp.concatenate([cols[k] for k in ("fq", "fk", "fv", "sq", "sk", "sv", "z", "xbc")],
                         axis=1).astype(BF16)
    ws = jnp.concatenate([cols["dt"], cols["ff"],
                          jnp.zeros((D_MODEL, LANES - SSD_HEADS - FOX_HEADS), F32)], axis=1).astype(BF16)

    main2, small2 = _inproj(x2, mix_g[None, :], wm, ws, ssd_conv_w, ssd_conv_b[None, :], s)
    main3 = main2.reshape(b, s, N_MAIN)
    small3 = small2.reshape(b, s, LANES)

    c_full, cp = _fox_gate(small3, _pad_lanes(fox_f_bias, SMALL_FF))
    c8 = c_full[:, :, SMALL_FF:SMALL_FF + FOX_HEADS]
    hps = 2 * FOX_PAIRS_PER_STEP
    cq = c8.reshape(b, s, FOX_HEADS // hps, hps).transpose(0, 2, 1, 3)
    y_fox = _fox_attention(main3, cp, cq, fox_out_g[None, :])
    y_sb = _sb_attention(main3, sb_out_g[None, :])

    e01 = (jnp.arange(2 * LANES)[:, None] % LANES == (jnp.arange(D_SSD)[None, :] // HEAD_DIM)).astype(BF16)
    y_ssd = _ssd(main3, small3, _pad_lanes(ssd_dt_bias, SMALL_DT), _pad_lanes(ssd_a_log, SMALL_DT),
                 jnp.repeat(ssd_d, HEAD_DIM)[None, :], ssd_norm_g[None, :], e01)

    t = b * s
    wo = w_out.astype(BF16)
    return _ffn(x2, y_fox.reshape(t, D_FOX), y_sb.reshape(t, D_SB), y_ssd.reshape(t, D_SSD),
                wo[:D_FOX], wo[D_FOX:D_FOX + D_SB], wo[D_FOX + D_SB:],
                ffn_g[None, :], w_up.astype(BF16), ffn_conv_w,
                ffn_conv_b[None, :], w_down.astype(BF16), final_g[None, :], s,
                final_norm)


def kernel(x, mix_norm_g, w_in, fox_f_bias, fox_out_g, sb_out_g, ssd_conv_w, ssd_conv_b, ssd_dt_bias,
           ssd_a_log, ssd_d, ssd_norm_g, w_out, ffn_norm_g, w_up, ffn_conv_w, ffn_conv_b, w_down,
           final_norm_g):
    b, s, d = x.shape
    depth = w_in.shape[0]
    x2 = x.reshape(b * s, d)
    for l in range(depth):
        x2 = _layer(x2, b, s, mix_norm_g[l], w_in[l], fox_f_bias[l], fox_out_g[l], sb_out_g[l],
                    ssd_conv_w[l], ssd_conv_b[l], ssd_dt_bias[l], ssd_a_log[l], ssd_d[l],
                    ssd_norm_g[l], w_out[l], ffn_norm_g[l], w_up[l], ffn_conv_w[l], ffn_conv_b[l],
                    w_down[l], final_norm_g, l == depth - 1)
    return x2.reshape(b, s, d)
```

```python
import functools

import jax
import jax.numpy as jnp
from jax import lax
from jax.experimental import pallas as pl
from jax.experimental.pallas import tpu as pltpu

F32 = jnp.float32
BF16 = jnp.bfloat16

D_MODEL = 1024
HEAD_DIM = 64
FOX_HEADS = 8
SB_HEADS = 8
SSD_HEADS = 16
SSD_GROUPS = 2
SSD_STATE = 128
SSD_CONV = 4
SSD_CHUNK = 128
D_FOX = FOX_HEADS * HEAD_DIM
D_SB = SB_HEADS * HEAD_DIM
D_SSD = SSD_HEADS * HEAD_DIM
D_BC = SSD_GROUPS * SSD_STATE
D_FF = 2816
FFN_CONV = 3
NORM_EPS = 1e-6
CP_TERMS = 3
EXP_UNDERFLOW = -104.0

LANES = 128
N_MAIN = 3 * D_FOX + 3 * D_SB + D_SSD + D_SSD + 2 * D_BC
OFF_FOX = 0
OFF_SB = 3 * D_FOX
OFF_Z = OFF_SB + 3 * D_SB
OFF_XS = OFF_Z + D_SSD
OFF_BC = OFF_XS + D_SSD
SMALL_DT = 0
SMALL_FF = SSD_HEADS

VMEM_LIMIT = 56 * 1024 * 1024

TM_PROJ = 512
TQ = 256
TQ_FOX = 512
SSD_CHUNKS_PER_STEP = 4
EXPAND_PAD = 16
ROW_BLOCK = 32
FOX_PAIRS_PER_STEP = 2
SB_PAIRS_PER_STEP = 4
FFN_FC = 256


def _cparams(sem):
    return pltpu.CompilerParams(dimension_semantics=sem, vmem_limit_bytes=VMEM_LIMIT)


def _layer_weight(rows, cols, layer):
    return pl.BlockSpec((None, rows, cols), lambda i: (layer, 0, 0), pipeline_mode=pl.Buffered(1))


def _split_bf16(x, n):
    parts, r = [], x
    for _ in range(n):
        p = r.astype(BF16)
        parts.append(p)
        r = r - p.astype(F32)
    return parts


def _dot(a, b):
    return jnp.dot(a, b, preferred_element_type=F32)


def _dot_nt(a, b):
    return lax.dot_general(a, b, (((1,), (1,)), ((), ())), preferred_element_type=F32)


def _dot_tn(a, b):
    return lax.dot_general(a, b, (((0,), (0,)), ((), ())), preferred_element_type=F32)


def _mask_dot_left(m01, x, n):
    out = None
    for p in _split_bf16(x, n):
        t = _dot(m01, p)
        out = t if out is None else out + t
    return out


def _softplus_neg_abs(x):
    return jnp.log(1.0 + jnp.exp(-jnp.abs(x)))


def _sigmoid(x):
    return 1.0 / (1.0 + jnp.exp(-x))


def _rms(x, g):
    ms = jnp.mean(x * x, axis=-1, keepdims=True)
    return x * lax.rsqrt(ms + NORM_EPS) * g


def _head_pair_norm(o2, g, lane):
    lo = lane < HEAD_DIM
    sq = o2 * o2
    ms0 = jnp.sum(jnp.where(lo, sq, 0.0), axis=-1, keepdims=True) * (1.0 / HEAD_DIM)
    ms1 = jnp.sum(jnp.where(lo, 0.0, sq), axis=-1, keepdims=True) * (1.0 / HEAD_DIM)
    ms = jnp.where(lo, ms0, ms1)
    return o2 * lax.rsqrt(ms + NORM_EPS) * g


def _inproj_kernel(x_ref, g_ref, wm_ref, ws_ref, cw_ref, cb_ref, main_ref, small_ref, carry_ref, u_ref,
                   *, tm, nc, tiles_per_seq):
    @pl.when(pl.program_id(0) % tiles_per_seq == 0)
    def _():
        carry_ref[...] = jnp.zeros_like(carry_ref)

    h = _rms(x_ref[...], g_ref[...]).astype(BF16)
    order = list(reversed(range(0, N_MAIN, nc)))

    def project(n):
        c = order[n]
        u_ref[n % 2, 8:8 + tm, :] = _dot(h, wm_ref[:, c:c + nc])
        if c >= OFF_XS:
            cc = slice(c - OFF_XS, c - OFF_XS + nc)
            u_ref[n % 2, 0:8, :] = carry_ref[:, cc]
            carry_ref[:, cc] = u_ref[n % 2, tm:tm + 8, :]

    project(0)
    for n, c in enumerate(order):
        if n + 1 < len(order):
            project(n + 1)
        acc = u_ref[n % 2, 8:8 + tm, :]
        if c >= OFF_XS:
            cc = slice(c - OFF_XS, c - OFF_XS + nc)
            acc = cb_ref[:, cc] + cw_ref[SSD_CONV - 1:SSD_CONV, cc] * acc
            for k in range(1, SSD_CONV):
                acc = acc + cw_ref[SSD_CONV - 1 - k:SSD_CONV - k, cc] * u_ref[n % 2, 8 - k:8 - k + tm, :]
        if c >= OFF_Z:
            acc = acc * _sigmoid(acc)
        main_ref[:, c:c + nc] = acc.astype(BF16)
    small_ref[...] = _dot(h, ws_ref[...])


def _inproj(x2, g, wm_all, ws_all, layer, cw, cb, seq):
    t = x2.shape[0]
    tm = TM_PROJ
    nc = 512
    n_conv = N_MAIN - OFF_XS
    return pl.pallas_call(
        functools.partial(_inproj_kernel, tm=tm, nc=nc, tiles_per_seq=seq // tm),
        grid=(t // tm,),
        in_specs=[
            pl.BlockSpec((tm, D_MODEL), lambda i: (i, 0)),
            pl.BlockSpec((1, D_MODEL), lambda i: (0, 0)),
            _layer_weight(D_MODEL, N_MAIN, layer),
            _layer_weight(D_MODEL, LANES, layer),
            pl.BlockSpec((SSD_CONV, n_conv), lambda i: (0, 0)),
            pl.BlockSpec((1, n_conv), lambda i: (0, 0)),
        ],
        out_specs=[
            pl.BlockSpec((tm, N_MAIN), lambda i: (i, 0)),
            pl.BlockSpec((tm, LANES), lambda i: (i, 0)),
        ],
        out_shape=[
            jax.ShapeDtypeStruct((t, N_MAIN), BF16),
            jax.ShapeDtypeStruct((t, LANES), F32),
        ],
        scratch_shapes=[
            pltpu.VMEM((8, n_conv), F32),
            pltpu.VMEM((2, tm + 8, nc), F32),
        ],
        compiler_params=_cparams(("arbitrary",)),
        name="inproj",
    )(x2, g, wm_all, ws_all, cw, cb)


def _fox_gate_kernel(s_ref, bias_ref, c_ref, cp_ref, carry_ref, *, tb):
    @pl.when(pl.program_id(1) == 0)
    def _():
        carry_ref[...] = jnp.zeros_like(carry_ref)

    xx = s_ref[0] + bias_ref[...]
    log_f = jnp.minimum(xx, 0.0) - _softplus_neg_abs(xx)
    row = lax.broadcasted_iota(jnp.int32, (tb, tb), 0)
    col = lax.broadcasted_iota(jnp.int32, (tb, tb), 1)
    tri = jnp.where(row >= col, 1.0, 0.0).astype(BF16)
    cum = _mask_dot_left(tri, log_f, 3) + carry_ref[...]
    c_ref[0] = cum
    carry_ref[...] = cum[tb - 1:tb, :]
    src = lax.broadcasted_iota(jnp.int32, (LANES, LANES), 0) - SMALL_FF
    dst = lax.broadcasted_iota(jnp.int32, (LANES, LANES), 1)
    head_ok = (src >= 0) & (src < FOX_HEADS)
    cp = None
    for j, part in enumerate(_split_bf16(-cum, CP_TERMS)):
        sel = jnp.where(head_ok & (dst == CP_TERMS * src + j), 1.0, 0.0).astype(BF16)
        t = _dot(part, sel)
        cp = t if cp is None else cp + t
    cp_ref[0] = cp.astype(BF16)


def _fox_gate(small3, bias_row):
    b, s, _ = small3.shape
    tb = 512
    blk = pl.BlockSpec((1, tb, LANES), lambda bi, i: (bi, i, 0))
    return pl.pallas_call(
        functools.partial(_fox_gate_kernel, tb=tb),
        grid=(b, s // tb),
        in_specs=[blk, pl.BlockSpec((1, LANES), lambda bi, i: (0, 0))],
        out_specs=[blk, blk],
        out_shape=[jax.ShapeDtypeStruct((b, s, LANES), F32),
                   jax.ShapeDtypeStruct((b, s, LANES), BF16)],
        scratch_shapes=[pltpu.VMEM((1, LANES), F32)],
        compiler_params=_cparams(("arbitrary", "arbitrary")),
        name="fox_gate",
    )(small3, bias_row)


def _fox_kernel(q_ref, k_ref, v_ref, cp_ref, cq_ref, g_ref, o_ref, vh_ref, m_ref, acc_ref, c_ref,
                alpha_ref, t_ref, p_ref, *, tq, npairs):
    p_idx = pl.program_id(1)
    i = pl.program_id(2)
    lane = lax.broadcasted_iota(jnp.int32, (1, LANES), 1)
    lo_half = lane < HEAD_DIM
    nheads = 2 * npairs
    pair = lambda h: slice((h // 2) * LANES, (h // 2 + 1) * LANES)

    @pl.when(i == 0)
    def _():
        for pr in range(npairs):
            v2 = v_ref[0, :, pair(2 * pr)]
            one = jnp.ones_like(v2)
            vh_ref[2 * pr] = jnp.where(lo_half, v2, one)
            vh_ref[2 * pr + 1] = jnp.where(lo_half, one, v2)

    q_aug = []
    for h in range(nheads):
        q2 = q_ref[0, :, pair(h)]
        own = lo_half if h % 2 == 0 else jnp.logical_not(lo_half)
        first = CP_TERMS * (nheads * p_idx + h)
        ones_at = jnp.where((lane >= first) & (lane < first + CP_TERMS), 1.0, 0.0).astype(BF16)
        q_aug.append(jnp.concatenate(
            [jnp.where(own, q2, jnp.zeros_like(q2)) * 0.125, jnp.broadcast_to(ones_at, (tq, LANES))],
            axis=1))
        c_lane = SMALL_FF + nheads * p_idx + h
        c_ref[h] = jnp.broadcast_to(
            jnp.sum(jnp.where(lane == c_lane, cq_ref[0], 0.0), axis=-1, keepdims=True), (tq, LANES))

    m_ref[...] = jnp.full_like(m_ref, -jnp.inf)
    acc_ref[...] = jnp.zeros_like(acc_ref)

    def tile(start, tk, masked):
        cp_t = cp_ref[0, pl.ds(start, tk), :]
        for h in range(nheads):
            k_aug = jnp.concatenate([k_ref[0, pl.ds(start, tk), pair(h)], cp_t], axis=1)
            t_ref[h, :, 0:tk] = _dot_nt(q_aug[h], k_aug)
        for h in range(nheads):
            for r in range(0, tq, ROW_BLOCK):
                rows = slice(r, r + ROW_BLOCK)
                t = t_ref[h, rows, 0:tk]
                if masked:
                    row = lax.broadcasted_iota(jnp.int32, (ROW_BLOCK, tk), 0) + r
                    col = lax.broadcasted_iota(jnp.int32, (ROW_BLOCK, tk), 1)
                    t = jnp.where(row >= col, t, -jnp.inf)
                m_prev = m_ref[h, rows, :]
                m_new = jnp.maximum(m_prev, jnp.max(t, axis=-1, keepdims=True) + c_ref[h, rows, :])
                shift = m_new - c_ref[h, rows, :]
                for c in range(0, tk, LANES):
                    p_ref[h, rows, c:c + LANES] = jnp.exp(t[:, c:c + LANES] - shift).astype(BF16)
                alpha_ref[h, rows, :] = jnp.exp(m_prev - m_new)
                m_ref[h, rows, :] = m_new
        for h in range(nheads):
            acc_ref[h] = alpha_ref[h] * acc_ref[h] + _dot(p_ref[h, :, 0:tk], vh_ref[h, pl.ds(start, tk), :])

    def body(j, carry):
        tile(pl.multiple_of(j * (2 * tq), 2 * tq), 2 * tq, False)
        return carry

    lax.fori_loop(0, i // 2, body, 0)

    @pl.when(i % 2 == 1)
    def _():
        tile(pl.multiple_of((i - 1) * tq, tq), tq, False)

    tile(pl.multiple_of(i * tq, tq), tq, True)

    for pr in range(npairs):
        a0, a1 = acc_ref[2 * pr], acc_ref[2 * pr + 1]
        num = jnp.where(lo_half, a0, a1)
        den = pltpu.roll(jnp.where(lo_half, a1, a0), HEAD_DIM, axis=1)
        o_ref[0, :, pair(2 * pr)] = _head_pair_norm(num / den, g_ref[:, pair(2 * pr)], lane).astype(BF16)


def _fox_attention(main3, cp, cq, gain):
    b, s, _ = main3.shape
    tq = TQ_FOX
    npairs = FOX_PAIRS_PER_STEP
    nheads = 2 * npairs
    w = npairs * LANES
    nsteps = D_FOX // w
    qb, kb, vb = OFF_FOX // w, (OFF_FOX + D_FOX) // w, (OFF_FOX + 2 * D_FOX) // w
    return pl.pallas_call(
        functools.partial(_fox_kernel, tq=tq, npairs=npairs),
        grid=(b, nsteps, s // tq),
        in_specs=[
            pl.BlockSpec((1, tq, w), lambda bi, p, i: (bi, i, qb + p)),
            pl.BlockSpec((1, s, w), lambda bi, p, i: (bi, 0, kb + p)),
            pl.BlockSpec((1, s, w), lambda bi, p, i: (bi, 0, vb + p)),
            pl.BlockSpec((1, s, LANES), lambda bi, p, i: (bi, 0, 0)),
            pl.BlockSpec((1, tq, LANES), lambda bi, p, i: (bi, i, 0)),
            pl.BlockSpec((1, w), lambda bi, p, i: (0, p)),
        ],
        out_specs=pl.BlockSpec((1, tq, w), lambda bi, p, i: (bi, i, p)),
        out_shape=jax.ShapeDtypeStruct((b, s, D_FOX), BF16),
        scratch_shapes=[
            pltpu.VMEM((nheads, s, LANES), BF16),
            pltpu.VMEM((nheads, tq, LANES), F32),
            pltpu.VMEM((nheads, tq, LANES), F32),
            pltpu.VMEM((nheads, tq, LANES), F32),
            pltpu.VMEM((nheads, tq, LANES), F32),
            pltpu.VMEM((nheads, tq, 2 * tq), F32),
            pltpu.VMEM((nheads, tq, 2 * tq), BF16),
        ],
        compiler_params=_cparams(("arbitrary", "arbitrary", "arbitrary")),
        name="fox_attn",
    )(main3, main3, main3, cp, cq, gain)


def _sb_kernel(q_ref, k_ref, v_ref, g_ref, o_ref, r_ref, acc_ref, z_ref, lb_ref, lk_ref, w_ref, *,
               tq, npairs):
    i = pl.program_id(2)
    lane = lax.broadcasted_iota(jnp.int32, (1, LANES), 1)
    nheads = 2 * npairs
    qh = []
    for pr in range(npairs):
        q2 = q_ref[0, :, pr * LANES:(pr + 1) * LANES]
        zero = jnp.zeros_like(q2)
        qh += [jnp.where(lane < HEAD_DIM, q2, zero) * 0.125,
               jnp.where(lane < HEAD_DIM, zero, q2) * 0.125]
    row = lax.broadcasted_iota(jnp.int32, (tq, tq), 0)
    col = lax.broadcasted_iota(jnp.int32, (tq, tq), 1)
    later = jnp.where(row > col, 1.0, 0.0).astype(BF16)

    r_ref[...] = jnp.zeros_like(r_ref)
    acc_ref[...] = jnp.zeros_like(acc_ref)

    def tile(j, masked):
        start = pl.multiple_of(j * tq, tq)
        pair = lambda hh: slice((hh // 2) * LANES, (hh // 2 + 1) * LANES)
        for hh in range(nheads):
            z_ref[hh] = _dot_nt(qh[hh], k_ref[0, pl.ds(start, tq), pair(hh)])
        r_prev = []
        for hh in range(nheads):
            z = z_ref[hh]
            log_beta = jnp.minimum(z, 0.0) - _softplus_neg_abs(z)
            log_keep = log_beta - z
            if masked:
                log_keep = jnp.where(col < row, log_keep, 0.0)
            lb_ref[hh] = log_beta
            lk_ref[hh] = log_keep.astype(BF16)
            r_prev.append(r_ref[hh])
            r_ref[hh] = r_prev[hh] + jnp.sum(log_keep, axis=-1, keepdims=True)
        for hh in range(nheads):
            z_ref[hh] = _dot(lk_ref[hh], later)
        for hh in range(nheads):
            log_w = lb_ref[hh] + z_ref[hh]
            chunks = []
            for c in range(0, tq, LANES):
                lw = log_w[:, c:c + LANES] + r_prev[hh]
                if masked:
                    row_c = lax.broadcasted_iota(jnp.int32, (tq, LANES), 0)
                    col_c = lax.broadcasted_iota(jnp.int32, (tq, LANES), 1) + c
                    lw = jnp.where(col_c < row_c, lw, -jnp.inf)
                chunks.append(jnp.exp(lw).astype(BF16))
            w_ref[hh] = jnp.concatenate(chunks, axis=1)
        for hh in range(nheads):
            acc_ref[hh] = acc_ref[hh] + _dot(w_ref[hh], v_ref[0, pl.ds(start, tq), pair(hh)])

    tile(i, True)

    def cond(carry):
        it, live = carry
        return jnp.logical_and(it < i, live)

    def body(carry):
        it, _ = carry
        tile(i - 1 - it, False)
        return it + 1, jnp.max(r_ref[...]) > EXP_UNDERFLOW

    lax.while_loop(cond, body, (0, True))

    for pr in range(npairs):
        ps = slice(pr * LANES, (pr + 1) * LANES)
        o2 = jnp.where(lane < HEAD_DIM, acc_ref[2 * pr], acc_ref[2 * pr + 1])
        o_ref[0, :, ps] = _head_pair_norm(o2, g_ref[:, ps], lane).astype(BF16)


def _sb_attention(main3, gain):
    b, s, _ = main3.shape
    tq = TQ
    npairs = SB_PAIRS_PER_STEP
    nheads = 2 * npairs
    w = npairs * LANES
    nsteps = D_SB // w
    qb, kb, vb = OFF_SB // w, (OFF_SB + D_SB) // w, (OFF_SB + 2 * D_SB) // w
    return pl.pallas_call(
        functools.partial(_sb_kernel, tq=tq, npairs=npairs),
        grid=(b, nsteps, s // tq),
        in_specs=[
            pl.BlockSpec((1, tq, w), lambda bi, p, i: (bi, i, qb + p)),
            pl.BlockSpec((1, s, w), lambda bi, p, i: (bi, 0, kb + p)),
            pl.BlockSpec((1, s, w), lambda bi, p, i: (bi, 0, vb + p)),
            pl.BlockSpec((1, w), lambda bi, p, i: (0, p)),
        ],
        out_specs=pl.BlockSpec((1, tq, w), lambda bi, p, i: (bi, i, p)),
        out_shape=jax.ShapeDtypeStruct((b, s, D_SB), BF16),
        scratch_shapes=[
            pltpu.VMEM((nheads, tq, LANES), F32),
            pltpu.VMEM((nheads, tq, LANES), F32),
            pltpu.VMEM((nheads, tq, tq), F32),
            pltpu.VMEM((nheads, tq, tq), F32),
            pltpu.VMEM((nheads, tq, tq), BF16),
            pltpu.VMEM((nheads, tq, tq), BF16),
        ],
        compiler_params=_cparams(("arbitrary", "arbitrary", "arbitrary")),
        name="sb_attn",
    )(main3, main3, main3, gain)


def _ssd_kernel(z_ref, xs_ref, bc_ref, dt_ref, dtb_ref, alog_ref, dexp_ref, ng_ref, e_ref, o_ref, state_ref):
    L = SSD_CHUNK
    hpg = SSD_HEADS // SSD_GROUPS
    gw = hpg * HEAD_DIM

    @pl.when(pl.program_id(1) == 0)
    def _():
        state_ref[...] = jnp.zeros_like(state_ref)

    def chunk(rows):
        xs = xs_ref[0, rows, :].astype(F32)
        bc = bc_ref[0, rows, :]

        xdt_raw = dt_ref[0, rows, :] + dtb_ref[...]
        dt = jnp.maximum(xdt_raw, 0.0) + _softplus_neg_abs(xdt_raw)
        a = -jnp.exp(alog_ref[...])
        da = dt * a
        row = lax.broadcasted_iota(jnp.int32, (L, L), 0)
        col = lax.broadcasted_iota(jnp.int32, (L, L), 1)
        causal = row >= col
        tri = jnp.where(causal, 1.0, 0.0).astype(BF16)
        a_cs = _mask_dot_left(tri, da, 3)
        a_cs_t = a_cs.T
        a_last = a_cs[L - 1:L, :]

        per_head = jnp.concatenate([dt, jnp.exp(a_cs), jnp.exp(a_last - a_cs),
                                    jnp.broadcast_to(jnp.exp(a_last), (EXPAND_PAD, LANES))], axis=0)
        wide = _dot(jnp.concatenate(_split_bf16(per_head, 2), axis=1), e_ref[...])
        dt_x = wide[0:L]
        dec_in_x = wide[L:2 * L]
        dec_end_x = wide[2 * L:3 * L]
        chunk_dec_x = wide[3 * L:3 * L + 1]

        xdt = xs * dt_x
        xdt_b = xdt.astype(BF16)
        xend_b = (xdt * dec_end_x).astype(BF16)

        lane = lax.broadcasted_iota(jnp.int32, (1, LANES), 1)
        ys = []
        for g in range(SSD_GROUPS):
            b_g = bc[:, g * SSD_STATE:(g + 1) * SSD_STATE]
            c_g = bc[:, D_BC + g * SSD_STATE:D_BC + (g + 1) * SSD_STATE]
            cb = _dot_nt(c_g, b_g)
            st = state_ref[g]
            y_off = _dot(c_g, st.astype(BF16)) * dec_in_x[:, g * gw:(g + 1) * gw]
            y_diag = []
            for pr in range(hpg // 2):
                outs = []
                for hh in range(2):
                    h = g * hpg + 2 * pr + hh
                    seg = jnp.exp(jnp.where(causal, a_cs[:, h:h + 1] - a_cs_t[h:h + 1, :], -jnp.inf))
                    m = (cb * seg).astype(BF16)
                    lo = (g * hpg + 2 * pr) * HEAD_DIM
                    outs.append(_dot(m, xdt_b[:, lo:lo + LANES]))
                y_diag.append(jnp.where(lane < HEAD_DIM, outs[0], outs[1]))
            y_g = jnp.concatenate(y_diag, axis=-1) + y_off
            ys.append(y_g)
            state_ref[g] = (chunk_dec_x[:, g * gw:(g + 1) * gw] * st
                            + _dot_tn(b_g, xend_b[:, g * gw:(g + 1) * gw]))

        gate = z_ref[0, rows, :].astype(F32)
        for g in range(SSD_GROUPS):
            sl = slice(g * gw, (g + 1) * gw)
            y_g = (ys[g] + xs[:, sl] * dexp_ref[:, sl]) * gate[:, sl]
            o_ref[0, rows, sl] = _rms(y_g, ng_ref[:, sl]).astype(BF16)

    for sub in range(SSD_CHUNKS_PER_STEP):
        chunk(slice(sub * L, (sub + 1) * L))


def _ssd(main3, small3, dtb, alog, dexp, ng, e01):
    b, s, _ = main3.shape
    L = SSD_CHUNK * SSD_CHUNKS_PER_STEP
    zb, xb, bcb = OFF_Z // D_SSD, OFF_XS // D_SSD, OFF_BC // (2 * D_BC)
    const = lambda shape: pl.BlockSpec(shape, lambda bi, c: (0,) * len(shape))
    return pl.pallas_call(
        _ssd_kernel,
        grid=(b, s // L),
        in_specs=[
            pl.BlockSpec((1, L, D_SSD), lambda bi, c: (bi, c, zb)),
            pl.BlockSpec((1, L, D_SSD), lambda bi, c: (bi, c, xb)),
            pl.BlockSpec((1, L, 2 * D_BC), lambda bi, c: (bi, c, bcb)),
            pl.BlockSpec((1, L, LANES), lambda bi, c: (bi, c, 0)),
            const((1, LANES)), const((1, LANES)),
            const((1, D_SSD)), const((1, D_SSD)),
            const((2 * LANES, D_SSD)),
        ],
        out_specs=pl.BlockSpec((1, L, D_SSD), lambda bi, c: (bi, c, 0)),
        out_shape=jax.ShapeDtypeStruct((b, s, D_SSD), BF16),
        scratch_shapes=[pltpu.VMEM((SSD_GROUPS, SSD_STATE, D_SSD // SSD_GROUPS), F32)],
        compiler_params=_cparams(("arbitrary", "arbitrary")),
        name="ssd",
    )(main3, main3, main3, small3, dtb, alog, dexp, ng, e01)


def _ffn_kernel(x_ref, yf_ref, ys_ref, yd_ref, wo_ref, g_ref, wu_ref, cw_ref, cb_ref, wd_ref, fg_ref, o_ref,
                carry_ref, u_ref, act_ref, *, tm, tiles_per_seq, final_norm):
    fc = FFN_FC
    nchunk = D_FF // fc

    @pl.when(pl.program_id(0) % tiles_per_seq == 0)
    def _():
        carry_ref[...] = jnp.zeros_like(carry_ref)

    x = (x_ref[...] + _dot(yf_ref[...], wo_ref[0:D_FOX, :])
         + _dot(ys_ref[...], wo_ref[D_FOX:D_FOX + D_SB, :])
         + _dot(yd_ref[...], wo_ref[D_FOX + D_SB:D_FOX + D_SB + D_SSD, :]))
    h = _rms(x, g_ref[...]).astype(BF16)

    def cols(c):
        return slice(c * fc, (c + 1) * fc), slice(D_FF + c * fc, D_FF + (c + 1) * fc)

    def up(c):
        for half, cs in enumerate(cols(c)):
            hs = slice(half * fc, (half + 1) * fc)
            u_ref[c % 2, 0:8, hs] = carry_ref[:, cs]
            u_ref[c % 2, 8:8 + tm, hs] = _dot(h, wu_ref[:, cs])
            carry_ref[:, cs] = u_ref[c % 2, tm:tm + 8, hs]

    up(0)
    for c in range(nchunk):
        if c + 1 < nchunk:
            up(c + 1)
        ys = []
        for half, cs in enumerate(cols(c)):
            hs = slice(half * fc, (half + 1) * fc)
            y = cb_ref[:, cs] + cw_ref[FFN_CONV - 1:FFN_CONV, cs] * u_ref[c % 2, 8:8 + tm, hs]
            for k in range(1, FFN_CONV):
                y = y + cw_ref[FFN_CONV - 1 - k:FFN_CONV - k, cs] * u_ref[c % 2, 8 - k:8 - k + tm, hs]
            ys.append(y)
        gate, val = ys
        act_ref[:, c * fc:(c + 1) * fc] = (gate * _sigmoid(gate) * val).astype(BF16)
    out = x + _dot(act_ref[...], wd_ref[...])
    if final_norm:
        out = _rms(out, fg_ref[...])
    o_ref[...] = out


def _ffn(x2, yf, ys, yd, wo_all, g, wu_all, cw, cb, wd_all, fg, layer, seq, final_norm):
    t = x2.shape[0]
    tm = TM_PROJ
    tok = lambda n: pl.BlockSpec((tm, n), lambda i: (i, 0))
    row = lambda n: pl.BlockSpec((1, n), lambda i: (0, 0))
    return pl.pallas_call(
        functools.partial(_ffn_kernel, tm=tm, tiles_per_seq=seq // tm, final_norm=final_norm),
        grid=(t // tm,),
        in_specs=[
            tok(D_MODEL), tok(D_FOX), tok(D_SB), tok(D_SSD),
            _layer_weight(D_FOX + D_SB + D_SSD, D_MODEL, layer),
            row(D_MODEL),
            _layer_weight(D_MODEL, 2 * D_FF, layer),
            pl.BlockSpec((FFN_CONV, 2 * D_FF), lambda i: (0, 0)),
            row(2 * D_FF),
            _layer_weight(D_FF, D_MODEL, layer),
            row(D_MODEL),
        ],
        out_specs=tok(D_MODEL),
        out_shape=jax.ShapeDtypeStruct((t, D_MODEL), F32),
        scratch_shapes=[
            pltpu.VMEM((8, 2 * D_FF), F32),
            pltpu.VMEM((2, tm + 8, 2 * FFN_FC), F32),
            pltpu.VMEM((tm, D_FF), BF16),
        ],
        compiler_params=_cparams(("arbitrary",)),
        name="ffn",
    )(x2, yf, ys, yd, wo_all, g, wu_all, cw, cb, wd_all, fg)


def _pad_lanes(v, offset):
    return jnp.zeros((1, LANES), F32).at[0, offset:offset + v.shape[0]].set(v.astype(F32))


def _layer(x2, b, s, layer, wm_all, ws_all, wo_all, wu_all, wd_all, mix_g, fox_f_bias, fox_out_g, sb_out_g,
           ssd_conv_w, ssd_conv_b, ssd_dt_bias, ssd_a_log, ssd_d, ssd_norm_g, ffn_g, ffn_conv_w, ffn_conv_b,
           final_g, final_norm):
    main2, small2 = _inproj(x2, mix_g[None, :], wm_all, ws_all, layer, ssd_conv_w, ssd_conv_b[None, :], s)
    main3 = main2.reshape(b, s, N_MAIN)
    small3 = small2.reshape(b, s, LANES)

    c_full, cp = _fox_gate(small3, _pad_lanes(fox_f_bias, SMALL_FF))
    y_fox = _fox_attention(main3, cp, c_full, fox_out_g[None, :])
    y_sb = _sb_attention(main3, sb_out_g[None, :])

    e01 = (jnp.arange(2 * LANES)[:, None] % LANES == (jnp.arange(D_SSD)[None, :] // HEAD_DIM)).astype(BF16)
    y_ssd = _ssd(main3, small3, _pad_lanes(ssd_dt_bias, SMALL_DT), _pad_lanes(ssd_a_log, SMALL_DT),
                 jnp.repeat(ssd_d, HEAD_DIM)[None, :], ssd_norm_g[None, :], e01)

    t = b * s
    return _ffn(x2, y_fox.reshape(t, D_FOX), y_sb.reshape(t, D_SB), y_ssd.reshape(t, D_SSD), wo_all,
                ffn_g[None, :], wu_all, ffn_conv_w, ffn_conv_b[None, :], wd_all, final_g[None, :],
                layer, s, final_norm)


def kernel(x, mix_norm_g, w_in, fox_f_bias, fox_out_g, sb_out_g, ssd_conv_w, ssd_conv_b, ssd_dt_bias,
           ssd_a_log, ssd_d, ssd_norm_g, w_out, ffn_norm_g, w_up, ffn_conv_w, ffn_conv_b, w_down,
           final_norm_g):
    b, s, d = x.shape
    depth = w_in.shape[0]
    ff0 = 3 * D_FOX
    dt0 = N_MAIN + FOX_HEADS
    wm_all = jnp.concatenate([w_in[:, :, :ff0], w_in[:, :, ff0 + FOX_HEADS:dt0]], axis=2).astype(BF16)
    ws_all = jnp.concatenate([w_in[:, :, dt0:dt0 + SSD_HEADS], w_in[:, :, ff0:ff0 + FOX_HEADS],
                              jnp.zeros((depth, D_MODEL, LANES - SSD_HEADS - FOX_HEADS), F32)],
                             axis=2).astype(BF16)
    wo_all, wu_all, wd_all = w_out.astype(BF16), w_up.astype(BF16), w_down.astype(BF16)
    x2 = x.reshape(b * s, d)
    for l in range(depth):
        x2 = _layer(x2, b, s, l, wm_all, ws_all, wo_all, wu_all, wd_all, mix_norm_g[l], fox_f_bias[l],
                    fox_out_g[l], sb_out_g[l], ssd_conv_w[l], ssd_conv_b[l], ssd_dt_bias[l], ssd_a_log[l],
                    ssd_d[l], ssd_norm_g[l], ffn_norm_g[l], ffn_conv_w[l], ffn_conv_b[l], final_norm_g,
                    l == depth - 1)
    return x2.reshape(b, s, d)
```

```python
import functools

import jax
import jax.numpy as jnp
from jax import lax
from jax.experimental import pallas as pl
from jax.experimental.pallas import tpu as pltpu

F32 = jnp.float32
BF16 = jnp.bfloat16

D_MODEL = 1024
HEAD_DIM = 64
FOX_HEADS = 8
SB_HEADS = 8
SSD_HEADS = 16
SSD_GROUPS = 2
SSD_STATE = 128
SSD_CONV = 4
SSD_CHUNK = 128
D_FOX = FOX_HEADS * HEAD_DIM
D_SB = SB_HEADS * HEAD_DIM
D_SSD = SSD_HEADS * HEAD_DIM
D_BC = SSD_GROUPS * SSD_STATE
D_FF = 2816
FFN_CONV = 3
NORM_EPS = 1e-6
CP_TERMS = 3
EXP_UNDERFLOW = -104.0

LANES = 128
N_MAIN = 3 * D_FOX + 3 * D_SB + D_SSD + D_SSD + 2 * D_BC
OFF_FOX = 0
OFF_SB = 3 * D_FOX
OFF_Z = OFF_SB + 3 * D_SB
OFF_XS = OFF_Z + D_SSD
OFF_BC = OFF_XS + D_SSD
SMALL_DT = 0
SMALL_FF = SSD_HEADS

VMEM_LIMIT = 56 * 1024 * 1024

TM_PROJ = 512
TQ = 256
TQ_FOX = 512
SSD_CHUNKS_PER_STEP = 8
EXPAND_PAD = 16
ROW_BLOCK = 32
FOX_PAIRS_PER_STEP = 2
SB_PAIRS_PER_STEP = 4
FFN_FC = 256


def _cparams(sem):
    return pltpu.CompilerParams(dimension_semantics=sem, vmem_limit_bytes=VMEM_LIMIT)


def _layer_weight(rows, cols, layer):
    return pl.BlockSpec((None, rows, cols), lambda i: (layer, 0, 0), pipeline_mode=pl.Buffered(1))


def _split_bf16(x, n):
    parts, r = [], x
    for _ in range(n):
        p = r.astype(BF16)
        parts.append(p)
        r = r - p.astype(F32)
    return parts


def _dot(a, b):
    return jnp.dot(a, b, preferred_element_type=F32)


def _dot_nt(a, b):
    return lax.dot_general(a, b, (((1,), (1,)), ((), ())), preferred_element_type=F32)


def _dot_tn(a, b):
    return lax.dot_general(a, b, (((0,), (0,)), ((), ())), preferred_element_type=F32)


def _mask_dot_left(m01, x, n):
    out = None
    for p in _split_bf16(x, n):
        t = _dot(m01, p)
        out = t if out is None else out + t
    return out


def _softplus_neg_abs(x):
    return jnp.log(1.0 + jnp.exp(-jnp.abs(x)))


def _sigmoid(x):
    return 1.0 / (1.0 + jnp.exp(-x))


def _rms(x, g):
    ms = jnp.mean(x * x, axis=-1, keepdims=True)
    return x * lax.rsqrt(ms + NORM_EPS) * g


def _head_pair_norm(o2, g, lane):
    lo = lane < HEAD_DIM
    sq = o2 * o2
    ms0 = jnp.sum(jnp.where(lo, sq, 0.0), axis=-1, keepdims=True) * (1.0 / HEAD_DIM)
    ms1 = jnp.sum(jnp.where(lo, 0.0, sq), axis=-1, keepdims=True) * (1.0 / HEAD_DIM)
    ms = jnp.where(lo, ms0, ms1)
    return o2 * lax.rsqrt(ms + NORM_EPS) * g


def _inproj_kernel(x_ref, g_ref, wm_ref, ws_ref, cw_ref, cb_ref, main_ref, small_ref, carry_ref, u_ref,
                   *, tm, nc, tiles_per_seq):
    @pl.when(pl.program_id(0) % tiles_per_seq == 0)
    def _():
        carry_ref[...] = jnp.zeros_like(carry_ref)

    h = _rms(x_ref[...], g_ref[...]).astype(BF16)
    order = list(reversed(range(0, N_MAIN, nc)))

    def project(n):
        c = order[n]
        u_ref[n % 2, 8:8 + tm, :] = _dot(h, wm_ref[:, c:c + nc])
        if c >= OFF_XS:
            cc = slice(c - OFF_XS, c - OFF_XS + nc)
            u_ref[n % 2, 0:8, :] = carry_ref[:, cc]
            carry_ref[:, cc] = u_ref[n % 2, tm:tm + 8, :]

    project(0)
    for n, c in enumerate(order):
        if n + 1 < len(order):
            project(n + 1)
        acc = u_ref[n % 2, 8:8 + tm, :]
        if c >= OFF_XS:
            cc = slice(c - OFF_XS, c - OFF_XS + nc)
            acc = cb_ref[:, cc] + cw_ref[SSD_CONV - 1:SSD_CONV, cc] * acc
            for k in range(1, SSD_CONV):
                acc = acc + cw_ref[SSD_CONV - 1 - k:SSD_CONV - k, cc] * u_ref[n % 2, 8 - k:8 - k + tm, :]
        if c >= OFF_Z:
            acc = acc * _sigmoid(acc)
        main_ref[:, c:c + nc] = acc.astype(BF16)
    small_ref[...] = _dot(h, ws_ref[...])


def _inproj(x2, g, wm_all, ws_all, layer, cw, cb, seq):
    t = x2.shape[0]
    tm = TM_PROJ
    nc = 512
    n_conv = N_MAIN - OFF_XS
    return pl.pallas_call(
        functools.partial(_inproj_kernel, tm=tm, nc=nc, tiles_per_seq=seq // tm),
        grid=(t // tm,),
        in_specs=[
            pl.BlockSpec((tm, D_MODEL), lambda i: (i, 0)),
            pl.BlockSpec((1, D_MODEL), lambda i: (0, 0)),
            _layer_weight(D_MODEL, N_MAIN, layer),
            _layer_weight(D_MODEL, LANES, layer),
            pl.BlockSpec((SSD_CONV, n_conv), lambda i: (0, 0)),
            pl.BlockSpec((1, n_conv), lambda i: (0, 0)),
        ],
        out_specs=[
            pl.BlockSpec((tm, N_MAIN), lambda i: (i, 0)),
            pl.BlockSpec((tm, LANES), lambda i: (i, 0)),
        ],
        out_shape=[
            jax.ShapeDtypeStruct((t, N_MAIN), BF16),
            jax.ShapeDtypeStruct((t, LANES), F32),
        ],
        scratch_shapes=[
            pltpu.VMEM((8, n_conv), F32),
            pltpu.VMEM((2, tm + 8, nc), F32),
        ],
        compiler_params=_cparams(("arbitrary",)),
        name="inproj",
    )(x2, g, wm_all, ws_all, cw, cb)


def _fox_gate_kernel(s_ref, bias_ref, c_ref, cp_ref, carry_ref, *, tb):
    @pl.when(pl.program_id(1) == 0)
    def _():
        carry_ref[...] = jnp.zeros_like(carry_ref)

    xx = s_ref[0] + bias_ref[...]
    log_f = jnp.minimum(xx, 0.0) - _softplus_neg_abs(xx)
    row = lax.broadcasted_iota(jnp.int32, (tb, tb), 0)
    col = lax.broadcasted_iota(jnp.int32, (tb, tb), 1)
    tri = jnp.where(row >= col, 1.0, 0.0).astype(BF16)
    cum = _mask_dot_left(tri, log_f, 3) + carry_ref[...]
    c_ref[0] = cum
    carry_ref[...] = cum[tb - 1:tb, :]
    src = lax.broadcasted_iota(jnp.int32, (LANES, LANES), 0) - SMALL_FF
    dst = lax.broadcasted_iota(jnp.int32, (LANES, LANES), 1)
    head_ok = (src >= 0) & (src < FOX_HEADS)
    cp = None
    for j, part in enumerate(_split_bf16(-cum, CP_TERMS)):
        sel = jnp.where(head_ok & (dst == CP_TERMS * src + j), 1.0, 0.0).astype(BF16)
        t = _dot(part, sel)
        cp = t if cp is None else cp + t
    cp_ref[0] = cp.astype(BF16)


def _fox_gate(small3, bias_row):
    b, s, _ = small3.shape
    tb = 512
    blk = pl.BlockSpec((1, tb, LANES), lambda bi, i: (bi, i, 0))
    return pl.pallas_call(
        functools.partial(_fox_gate_kernel, tb=tb),
        grid=(b, s // tb),
        in_specs=[blk, pl.BlockSpec((1, LANES), lambda bi, i: (0, 0))],
        out_specs=[blk, blk],
        out_shape=[jax.ShapeDtypeStruct((b, s, LANES), F32),
                   jax.ShapeDtypeStruct((b, s, LANES), BF16)],
        scratch_shapes=[pltpu.VMEM((1, LANES), F32)],
        compiler_params=_cparams(("arbitrary", "arbitrary")),
        name="fox_gate",
    )(small3, bias_row)


def _fox_kernel(q_ref, k_ref, v_ref, cp_ref, cq_ref, g_ref, o_ref, vh_ref, m_ref, acc_ref, c_ref,
                alpha_ref, t_ref, p_ref, *, tq, npairs):
    p_idx = pl.program_id(1)
    i = pl.program_id(2)
    lane = lax.broadcasted_iota(jnp.int32, (1, LANES), 1)
    lo_half = lane < HEAD_DIM
    nheads = 2 * npairs
    pair = lambda h: slice((h // 2) * LANES, (h // 2 + 1) * LANES)

    @pl.when(i == 0)
    def _():
        for pr in range(npairs):
            v2 = v_ref[0, :, pair(2 * pr)]
            one = jnp.ones_like(v2)
            vh_ref[2 * pr] = jnp.where(lo_half, v2, one)
            vh_ref[2 * pr + 1] = jnp.where(lo_half, one, v2)

    q_aug = []
    for h in range(nheads):
        q2 = q_ref[0, :, pair(h)]
        own = lo_half if h % 2 == 0 else jnp.logical_not(lo_half)
        first = CP_TERMS * (nheads * p_idx + h)
        ones_at = jnp.where((lane >= first) & (lane < first + CP_TERMS), 1.0, 0.0).astype(BF16)
        q_aug.append(jnp.concatenate(
            [jnp.where(own, q2, jnp.zeros_like(q2)) * 0.125, jnp.broadcast_to(ones_at, (tq, LANES))],
            axis=1))
        c_lane = SMALL_FF + nheads * p_idx + h
        c_ref[h] = jnp.broadcast_to(
            jnp.sum(jnp.where(lane == c_lane, cq_ref[0], 0.0), axis=-1, keepdims=True), (tq, LANES))

    m_ref[...] = jnp.full_like(m_ref, -jnp.inf)
    acc_ref[...] = jnp.zeros_like(acc_ref)

    def tile(start, tk, masked):
        cp_t = cp_ref[0, pl.ds(start, tk), :]
        for h in range(nheads):
            k_aug = jnp.concatenate([k_ref[0, pl.ds(start, tk), pair(h)], cp_t], axis=1)
            t_ref[h, :, 0:tk] = _dot_nt(q_aug[h], k_aug)
        for h in range(nheads):
            for r in range(0, tq, ROW_BLOCK):
                rows = slice(r, r + ROW_BLOCK)
                t = t_ref[h, rows, 0:tk]
                if masked:
                    row = lax.broadcasted_iota(jnp.int32, (ROW_BLOCK, tk), 0) + r
                    col = lax.broadcasted_iota(jnp.int32, (ROW_BLOCK, tk), 1)
                    t = jnp.where(row >= col, t, -jnp.inf)
                m_prev = m_ref[h, rows, :]
                m_new = jnp.maximum(m_prev, jnp.max(t, axis=-1, keepdims=True) + c_ref[h, rows, :])
                shift = m_new - c_ref[h, rows, :]
                for c in range(0, tk, LANES):
                    p_ref[h, rows, c:c + LANES] = jnp.exp(t[:, c:c + LANES] - shift).astype(BF16)
                alpha_ref[h, rows, :] = jnp.exp(m_prev - m_new)
                m_ref[h, rows, :] = m_new
        for h in range(nheads):
            acc_ref[h] = alpha_ref[h] * acc_ref[h] + _dot(p_ref[h, :, 0:tk], vh_ref[h, pl.ds(start, tk), :])

    def body(j, carry):
        tile(pl.multiple_of(j * (2 * tq), 2 * tq), 2 * tq, False)
        return carry

    lax.fori_loop(0, i // 2, body, 0)

    @pl.when(i % 2 == 1)
    def _():
        tile(pl.multiple_of((i - 1) * tq, tq), tq, False)

    tile(pl.multiple_of(i * tq, tq), tq, True)

    for pr in range(npairs):
        a0, a1 = acc_ref[2 * pr], acc_ref[2 * pr + 1]
        num = jnp.where(lo_half, a0, a1)
        den = pltpu.roll(jnp.where(lo_half, a1, a0), HEAD_DIM, axis=1)
        o_ref[0, :, pair(2 * pr)] = _head_pair_norm(num / den, g_ref[:, pair(2 * pr)], lane).astype(BF16)


def _fox_attention(main3, cp, cq, gain):
    b, s, _ = main3.shape
    tq = TQ_FOX
    npairs = FOX_PAIRS_PER_STEP
    nheads = 2 * npairs
    w = npairs * LANES
    nsteps = D_FOX // w
    qb, kb, vb = OFF_FOX // w, (OFF_FOX + D_FOX) // w, (OFF_FOX + 2 * D_FOX) // w
    return pl.pallas_call(
        functools.partial(_fox_kernel, tq=tq, npairs=npairs),
        grid=(b, nsteps, s // tq),
        in_specs=[
            pl.BlockSpec((1, tq, w), lambda bi, p, i: (bi, i, qb + p)),
            pl.BlockSpec((1, s, w), lambda bi, p, i: (bi, 0, kb + p)),
            pl.BlockSpec((1, s, w), lambda bi, p, i: (bi, 0, vb + p)),
            pl.BlockSpec((1, s, LANES), lambda bi, p, i: (bi, 0, 0)),
            pl.BlockSpec((1, tq, LANES), lambda bi, p, i: (bi, i, 0)),
            pl.BlockSpec((1, w), lambda bi, p, i: (0, p)),
        ],
        out_specs=pl.BlockSpec((1, tq, w), lambda bi, p, i: (bi, i, p)),
        out_shape=jax.ShapeDtypeStruct((b, s, D_FOX), BF16),
        scratch_shapes=[
            pltpu.VMEM((nheads, s, LANES), BF16),
            pltpu.VMEM((nheads, tq, LANES), F32),
            pltpu.VMEM((nheads, tq, LANES), F32),
            pltpu.VMEM((nheads, tq, LANES), F32),
            pltpu.VMEM((nheads, tq, LANES), F32),
            pltpu.VMEM((nheads, tq, 2 * tq), F32),
            pltpu.VMEM((nheads, tq, 2 * tq), BF16),
        ],
        compiler_params=_cparams(("arbitrary", "arbitrary", "arbitrary")),
        name="fox_attn",
    )(main3, main3, main3, cp, cq, gain)


def _sb_kernel(q_ref, k_ref, v_ref, g_ref, o_ref, r_ref, acc_ref, z_ref, lb_ref, lk_ref, w_ref, *,
               tq, npairs):
    i = pl.program_id(2)
    lane = lax.broadcasted_iota(jnp.int32, (1, LANES), 1)
    nheads = 2 * npairs
    qh = []
    for pr in range(npairs):
        q2 = q_ref[0, :, pr * LANES:(pr + 1) * LANES]
        zero = jnp.zeros_like(q2)
        qh += [jnp.where(lane < HEAD_DIM, q2, zero) * 0.125,
               jnp.where(lane < HEAD_DIM, zero, q2) * 0.125]
    row = lax.broadcasted_iota(jnp.int32, (tq, tq), 0)
    col = lax.broadcasted_iota(jnp.int32, (tq, tq), 1)
    later = jnp.where(row > col, 1.0, 0.0).astype(BF16)

    r_ref[...] = jnp.zeros_like(r_ref)
    acc_ref[...] = jnp.zeros_like(acc_ref)

    def tile(j, masked):
        start = pl.multiple_of(j * tq, tq)
        pair = lambda hh: slice((hh // 2) * LANES, (hh // 2 + 1) * LANES)
        for hh in range(nheads):
            z_ref[hh] = _dot_nt(qh[hh], k_ref[0, pl.ds(start, tq), pair(hh)])
        r_prev = []
        for hh in range(nheads):
            z = z_ref[hh]
            log_beta = jnp.minimum(z, 0.0) - _softplus_neg_abs(z)
            log_keep = log_beta - z
            if masked:
                log_keep = jnp.where(col < row, log_keep, 0.0)
            lb_ref[hh] = log_beta
            lk_ref[hh] = log_keep.astype(BF16)
            r_prev.append(r_ref[hh])
            r_ref[hh] = r_prev[hh] + jnp.sum(log_keep, axis=-1, keepdims=True)
        for hh in range(nheads):
            z_ref[hh] = _dot(lk_ref[hh], later)
        for hh in range(nheads):
            log_w = lb_ref[hh] + z_ref[hh]
            chunks = []
            for c in range(0, tq, LANES):
                lw = log_w[:, c:c + LANES] + r_prev[hh]
                if masked:
                    row_c = lax.broadcasted_iota(jnp.int32, (tq, LANES), 0)
                    col_c = lax.broadcasted_iota(jnp.int32, (tq, LANES), 1) + c
                    lw = jnp.where(col_c < row_c, lw, -jnp.inf)
                chunks.append(jnp.exp(lw).astype(BF16))
            w_ref[hh] = jnp.concatenate(chunks, axis=1)
        for hh in range(nheads):
            acc_ref[hh] = acc_ref[hh] + _dot(w_ref[hh], v_ref[0, pl.ds(start, tq), pair(hh)])

    tile(i, True)

    def cond(carry):
        it, live = carry
        return jnp.logical_and(it < i, live)

    def body(carry):
        it, _ = carry
        tile(i - 1 - it, False)
        return it + 1, jnp.max(r_ref[...]) > EXP_UNDERFLOW

    lax.while_loop(cond, body, (0, True))

    for pr in range(npairs):
        ps = slice(pr * LANES, (pr + 1) * LANES)
        o2 = jnp.where(lane < HEAD_DIM, acc_ref[2 * pr], acc_ref[2 * pr + 1])
        o_ref[0, :, ps] = _head_pair_norm(o2, g_ref[:, ps], lane).astype(BF16)


def _sb_attention(main3, gain):
    b, s, _ = main3.shape
    tq = TQ
    npairs = SB_PAIRS_PER_STEP
    nheads = 2 * npairs
    w = npairs * LANES
    nsteps = D_SB // w
    qb, kb, vb = OFF_SB // w, (OFF_SB + D_SB) // w, (OFF_SB + 2 * D_SB) // w
    return pl.pallas_call(
        functools.partial(_sb_kernel, tq=tq, npairs=npairs),
        grid=(b, nsteps, s // tq),
        in_specs=[
            pl.BlockSpec((1, tq, w), lambda bi, p, i: (bi, i, qb + p)),
            pl.BlockSpec((1, s, w), lambda bi, p, i: (bi, 0, kb + p)),
            pl.BlockSpec((1, s, w), lambda bi, p, i: (bi, 0, vb + p)),
            pl.BlockSpec((1, w), lambda bi, p, i: (0, p)),
        ],
        out_specs=pl.BlockSpec((1, tq, w), lambda bi, p, i: (bi, i, p)),
        out_shape=jax.ShapeDtypeStruct((b, s, D_SB), BF16),
        scratch_shapes=[
            pltpu.VMEM((nheads, tq, LANES), F32),
            pltpu.VMEM((nheads, tq, LANES), F32),
            pltpu.VMEM((nheads, tq, tq), F32),
            pltpu.VMEM((nheads, tq, tq), F32),
            pltpu.VMEM((nheads, tq, tq), BF16),
            pltpu.VMEM((nheads, tq, tq), BF16),
        ],
        compiler_params=_cparams(("arbitrary", "arbitrary", "arbitrary")),
        name="sb_attn",
    )(main3, main3, main3, gain)


def _ssd_kernel(z_ref, xs_ref, bc_ref, dt_ref, dtb_ref, alog_ref, dexp_ref, ng_ref, e_ref, o_ref, state_ref):
    L = SSD_CHUNK
    hpg = SSD_HEADS // SSD_GROUPS
    gw = hpg * HEAD_DIM

    @pl.when(pl.program_id(1) == 0)
    def _():
        state_ref[...] = jnp.zeros_like(state_ref)

    def chunk(rows):
        xdt_raw = dt_ref[0, rows, :] + dtb_ref[...]
        dt = jnp.maximum(xdt_raw, 0.0) + _softplus_neg_abs(xdt_raw)
        a = -jnp.exp(alog_ref[...])
        da = dt * a
        row = lax.broadcasted_iota(jnp.int32, (L, L), 0)
        col = lax.broadcasted_iota(jnp.int32, (L, L), 1)
        causal = row >= col
        tri = jnp.where(causal, 1.0, 0.0).astype(BF16)
        a_cs = _mask_dot_left(tri, da, 3)
        a_cs_t = a_cs.T
        a_last = a_cs[L - 1:L, :]

        per_head = jnp.concatenate([dt, jnp.exp(a_cs), jnp.exp(a_last - a_cs),
                                    jnp.broadcast_to(jnp.exp(a_last), (EXPAND_PAD, LANES))], axis=0)
        per_head = jnp.concatenate(_split_bf16(per_head, 2), axis=1)

        lane = lax.broadcasted_iota(jnp.int32, (1, LANES), 1)
        for g in range(SSD_GROUPS):
            sl = slice(g * gw, (g + 1) * gw)
            wide = _dot(per_head, e_ref[:, sl])
            dt_x = wide[0:L]
            dec_in_x = wide[L:2 * L]
            dec_end_x = wide[2 * L:3 * L]
            chunk_dec_x = wide[3 * L:3 * L + 1]

            xs = xs_ref[0, rows, sl].astype(F32)
            xdt = xs * dt_x
            xdt_b = xdt.astype(BF16)
            xend_b = (xdt * dec_end_x).astype(BF16)

            b_g = bc_ref[0, rows, g * SSD_STATE:(g + 1) * SSD_STATE]
            c_g = bc_ref[0, rows, D_BC + g * SSD_STATE:D_BC + (g + 1) * SSD_STATE]
            cb = _dot_nt(c_g, b_g)
            st = state_ref[g]
            y_off = _dot(c_g, st.astype(BF16)) * dec_in_x
            y_diag = []
            for pr in range(hpg // 2):
                outs = []
                for hh in range(2):
                    h = g * hpg + 2 * pr + hh
                    seg = jnp.exp(jnp.where(causal, a_cs[:, h:h + 1] - a_cs_t[h:h + 1, :], -jnp.inf))
                    m = (cb * seg).astype(BF16)
                    outs.append(_dot(m, xdt_b[:, pr * LANES:(pr + 1) * LANES]))
                y_diag.append(jnp.where(lane < HEAD_DIM, outs[0], outs[1]))
            state_ref[g] = chunk_dec_x * st + _dot_tn(b_g, xend_b)

            y_g = jnp.concatenate(y_diag, axis=-1) + y_off
            y_g = (y_g + xs * dexp_ref[:, sl]) * z_ref[0, rows, sl].astype(F32)
            o_ref[0, rows, sl] = _rms(y_g, ng_ref[:, sl]).astype(BF16)

    for sub in range(SSD_CHUNKS_PER_STEP):
        chunk(slice(sub * L, (sub + 1) * L))


def _ssd(main3, small3, dtb, alog, dexp, ng, e01):
    b, s, _ = main3.shape
    L = SSD_CHUNK * SSD_CHUNKS_PER_STEP
    zb, xb, bcb = OFF_Z // D_SSD, OFF_XS // D_SSD, OFF_BC // (2 * D_BC)
    const = lambda shape: pl.BlockSpec(shape, lambda bi, c: (0,) * len(shape))
    return pl.pallas_call(
        _ssd_kernel,
        grid=(b, s // L),
        in_specs=[
            pl.BlockSpec((1, L, D_SSD), lambda bi, c: (bi, c, zb)),
            pl.BlockSpec((1, L, D_SSD), lambda bi, c: (bi, c, xb)),
            pl.BlockSpec((1, L, 2 * D_BC), lambda bi, c: (bi, c, bcb)),
            pl.BlockSpec((1, L, LANES), lambda bi, c: (bi, c, 0)),
            const((1, LANES)), const((1, LANES)),
            const((1, D_SSD)), const((1, D_SSD)),
            const((2 * LANES, D_SSD)),
        ],
        out_specs=pl.BlockSpec((1, L, D_SSD), lambda bi, c: (bi, c, 0)),
        out_shape=jax.ShapeDtypeStruct((b, s, D_SSD), BF16),
        scratch_shapes=[pltpu.VMEM((SSD_GROUPS, SSD_STATE, D_SSD // SSD_GROUPS), F32)],
        compiler_params=_cparams(("arbitrary", "arbitrary")),
        name="ssd",
    )(main3, main3, main3, small3, dtb, alog, dexp, ng, e01)


def _ffn_kernel(x_ref, yf_ref, ys_ref, yd_ref, wo_ref, g_ref, wu_ref, cw_ref, cb_ref, wd_ref, fg_ref, o_ref,
                carry_ref, u_ref, act_ref, *, tm, tiles_per_seq, final_norm):
    fc = FFN_FC
    nchunk = D_FF // fc

    @pl.when(pl.program_id(0) % tiles_per_seq == 0)
    def _():
        carry_ref[...] = jnp.zeros_like(carry_ref)

    x = (x_ref[...] + _dot(yf_ref[...], wo_ref[0:D_FOX, :])
         + _dot(ys_ref[...], wo_ref[D_FOX:D_FOX + D_SB, :])
         + _dot(yd_ref[...], wo_ref[D_FOX + D_SB:D_FOX + D_SB + D_SSD, :]))
    h = _rms(x, g_ref[...]).astype(BF16)

    def cols(c):
        return slice(c * fc, (c + 1) * fc), slice(D_FF + c * fc, D_FF + (c + 1) * fc)

    def up(c):
        for half, cs in enumerate(cols(c)):
            hs = slice(half * fc, (half + 1) * fc)
            u_ref[c % 2, 0:8, hs] = carry_ref[:, cs]
            u_ref[c % 2, 8:8 + tm, hs] = _dot(h, wu_ref[:, cs])
            carry_ref[:, cs] = u_ref[c % 2, tm:tm + 8, hs]

    up(0)
    for c in range(nchunk):
        if c + 1 < nchunk:
            up(c + 1)
        ys = []
        for half, cs in enumerate(cols(c)):
            hs = slice(half * fc, (half + 1) * fc)
            y = cb_ref[:, cs] + cw_ref[FFN_CONV - 1:FFN_CONV, cs] * u_ref[c % 2, 8:8 + tm, hs]
            for k in range(1, FFN_CONV):
                y = y + cw_ref[FFN_CONV - 1 - k:FFN_CONV - k, cs] * u_ref[c % 2, 8 - k:8 - k + tm, hs]
            ys.append(y)
        gate, val = ys
        act_ref[:, c * fc:(c + 1) * fc] = (gate * _sigmoid(gate) * val).astype(BF16)
    out = x + _dot(act_ref[...], wd_ref[...])
    if final_norm:
        out = _rms(out, fg_ref[...])
    o_ref[...] = out


def _ffn(x2, yf, ys, yd, wo_all, g, wu_all, cw, cb, wd_all, fg, layer, seq, final_norm):
    t = x2.shape[0]
    tm = TM_PROJ
    tok = lambda n: pl.BlockSpec((tm, n), lambda i: (i, 0))
    row = lambda n: pl.BlockSpec((1, n), lambda i: (0, 0))
    return pl.pallas_call(
        functools.partial(_ffn_kernel, tm=tm, tiles_per_seq=seq // tm, final_norm=final_norm),
        grid=(t // tm,),
        in_specs=[
            tok(D_MODEL), tok(D_FOX), tok(D_SB), tok(D_SSD),
            _layer_weight(D_FOX + D_SB + D_SSD, D_MODEL, layer),
            row(D_MODEL),
            _layer_weight(D_MODEL, 2 * D_FF, layer),
            pl.BlockSpec((FFN_CONV, 2 * D_FF), lambda i: (0, 0)),
            row(2 * D_FF),
            _layer_weight(D_FF, D_MODEL, layer),
            row(D_MODEL),
        ],
        out_specs=tok(D_MODEL),
        out_shape=jax.ShapeDtypeStruct((t, D_MODEL), F32),
        scratch_shapes=[
            pltpu.VMEM((8, 2 * D_FF), F32),
            pltpu.VMEM((2, tm + 8, 2 * FFN_FC), F32),
            pltpu.VMEM((tm, D_FF), BF16),
        ],
        compiler_params=_cparams(("arbitrary",)),
        name="ffn",
    )(x2, yf, ys, yd, wo_all, g, wu_all, cw, cb, wd_all, fg)


def _pad_lanes(v, offset):
    return jnp.zeros((1, LANES), F32).at[0, offset:offset + v.shape[0]].set(v.astype(F32))


def _layer(x2, b, s, layer, wm_all, ws_all, wo_all, wu_all, wd_all, mix_g, fox_f_bias, fox_out_g, sb_out_g,
           ssd_conv_w, ssd_conv_b, ssd_dt_bias, ssd_a_log, ssd_d, ssd_norm_g, ffn_g, ffn_conv_w, ffn_conv_b,
           final_g, final_norm):
    main2, small2 = _inproj(x2, mix_g[None, :], wm_all, ws_all, layer, ssd_conv_w, ssd_conv_b[None, :], s)
    main3 = main2.reshape(b, s, N_MAIN)
    small3 = small2.reshape(b, s, LANES)

    c_full, cp = _fox_gate(small3, _pad_lanes(fox_f_bias, SMALL_FF))
    y_fox = _fox_attention(main3, cp, c_full, fox_out_g[None, :])
    y_sb = _sb_attention(main3, sb_out_g[None, :])

    e01 = (jnp.arange(2 * LANES)[:, None] % LANES == (jnp.arange(D_SSD)[None, :] // HEAD_DIM)).astype(BF16)
    y_ssd = _ssd(main3, small3, _pad_lanes(ssd_dt_bias, SMALL_DT), _pad_lanes(ssd_a_log, SMALL_DT),
                 jnp.repeat(ssd_d, HEAD_DIM)[None, :], ssd_norm_g[None, :], e01)

    t = b * s
    return _ffn(x2, y_fox.reshape(t, D_FOX), y_sb.reshape(t, D_SB), y_ssd.reshape(t, D_SSD), wo_all,
                ffn_g[None, :], wu_all, ffn_conv_w, ffn_conv_b[None, :], wd_all, final_g[None, :],
                layer, s, final_norm)


def kernel(x, mix_norm_g, w_in, fox_f_bias, fox_out_g, sb_out_g, ssd_conv_w, ssd_conv_b, ssd_dt_bias,
           ssd_a_log, ssd_d, ssd_norm_g, w_out, ffn_norm_g, w_up, ffn_conv_w, ffn_conv_b, w_down,
           final_norm_g):
    b, s, d = x.shape
    depth = w_in.shape[0]
    ff0 = 3 * D_FOX
    dt0 = N_MAIN + FOX_HEADS
    wm_all = jnp.concatenate([w_in[:, :, :ff0], w_in[:, :, ff0 + FOX_HEADS:dt0]], axis=2).astype(BF16)
    ws_all = jnp.concatenate([w_in[:, :, dt0:dt0 + SSD_HEADS], w_in[:, :, ff0:ff0 + FOX_HEADS],
                              jnp.zeros((depth, D_MODEL, LANES - SSD_HEADS - FOX_HEADS), F32)],
                             axis=2).astype(BF16)
    wo_all, wu_all, wd_all = w_out.astype(BF16), w_up.astype(BF16), w_down.astype(BF16)
    x2 = x.reshape(b * s, d)
    for l in range(depth):
        x2 = _layer(x2, b, s, l, wm_all, ws_all, wo_all, wu_all, wd_all, mix_norm_g[l], fox_f_bias[l],
                    fox_out_g[l], sb_out_g[l], ssd_conv_w[l], ssd_conv_b[l], ssd_dt_bias[l], ssd_a_log[l],
                    ssd_d[l], ssd_norm_g[l], ffn_norm_g[l], ffn_conv_w[l], ffn_conv_b[l], final_norm_g,
                    l == depth - 1)
    return x2.reshape(b, s, d)
```

```python
import functools

import jax
import jax.numpy as jnp
from jax import lax
from jax.experimental import pallas as pl
from jax.experimental.pallas import tpu as pltpu

F32 = jnp.float32
BF16 = jnp.bfloat16

D_MODEL = 1024
HEAD_DIM = 64
FOX_HEADS = 8
SB_HEADS = 8
SSD_HEADS = 16
SSD_GROUPS = 2
SSD_STATE = 128
SSD_CONV = 4
SSD_CHUNK = 128
D_FOX = FOX_HEADS * HEAD_DIM
D_SB = SB_HEADS * HEAD_DIM
D_SSD = SSD_HEADS * HEAD_DIM
D_BC = SSD_GROUPS * SSD_STATE
D_FF = 2816
FFN_CONV = 3
NORM_EPS = 1e-6
CP_TERMS = 3
EXP_UNDERFLOW = -104.0

LANES = 128
N_MAIN = 3 * D_FOX + 3 * D_SB + D_SSD + D_SSD + 2 * D_BC
OFF_FOX = 0
OFF_SB = 3 * D_FOX
OFF_Z = OFF_SB + 3 * D_SB
OFF_XS = OFF_Z + D_SSD
OFF_BC = OFF_XS + D_SSD
SMALL_DT = 0
SMALL_FF = SSD_HEADS

VMEM_LIMIT = 56 * 1024 * 1024

ATTN_SCALE = HEAD_DIM ** -0.5
TM_PROJ = 512
PROJ_CHUNK = 512
GATE_ROWS = 512
TQ = 256
TQ_FOX = 512
SSD_CHUNKS_PER_STEP = 8
EXPAND_PAD = 16
ROW_BLOCK = 32
FOX_PAIRS_PER_STEP = 2
SB_PAIRS_PER_STEP = 4
FFN_FC = 256


def _cparams(sem):
    return pltpu.CompilerParams(dimension_semantics=sem, vmem_limit_bytes=VMEM_LIMIT)


def _layer_weight(rows, cols, layer):
    return pl.BlockSpec((None, rows, cols), lambda i: (layer, 0, 0), pipeline_mode=pl.Buffered(1))


def _split_bf16(x, n):
    parts, r = [], x
    for _ in range(n):
        p = r.astype(BF16)
        parts.append(p)
        r = r - p.astype(F32)
    return parts


def _dot(a, b):
    return jnp.dot(a, b, preferred_element_type=F32)


def _dot_nt(a, b):
    return lax.dot_general(a, b, (((1,), (1,)), ((), ())), preferred_element_type=F32)


def _dot_tn(a, b):
    return lax.dot_general(a, b, (((0,), (0,)), ((), ())), preferred_element_type=F32)


def _mask_dot_left(m01, x, n):
    out = None
    for p in _split_bf16(x, n):
        t = _dot(m01, p)
        out = t if out is None else out + t
    return out


def _softplus_neg_abs(x):
    return jnp.log(1.0 + jnp.exp(-jnp.abs(x)))


def _sigmoid(x):
    return 1.0 / (1.0 + jnp.exp(-x))


def _rms(x, g):
    ms = jnp.mean(x * x, axis=-1, keepdims=True)
    return x * lax.rsqrt(ms + NORM_EPS) * g


def _head_pair_norm(o2, g, lane):
    lo = lane < HEAD_DIM
    sq = o2 * o2
    ms0 = jnp.sum(jnp.where(lo, sq, 0.0), axis=-1, keepdims=True) * (1.0 / HEAD_DIM)
    ms1 = jnp.sum(jnp.where(lo, 0.0, sq), axis=-1, keepdims=True) * (1.0 / HEAD_DIM)
    ms = jnp.where(lo, ms0, ms1)
    return o2 * lax.rsqrt(ms + NORM_EPS) * g


def _inproj_kernel(x_ref, g_ref, wm_ref, ws_ref, cw_ref, cb_ref, main_ref, small_ref, carry_ref, u_ref,
                   *, tm, nc, tiles_per_seq):
    @pl.when(pl.program_id(0) % tiles_per_seq == 0)
    def _():
        carry_ref[...] = jnp.zeros_like(carry_ref)

    h = _rms(x_ref[...], g_ref[...]).astype(BF16)
    order = list(reversed(range(0, N_MAIN, nc)))

    def project(n):
        c = order[n]
        u_ref[n % 2, 8:8 + tm, :] = _dot(h, wm_ref[:, c:c + nc])
        if c >= OFF_XS:
            cc = slice(c - OFF_XS, c - OFF_XS + nc)
            u_ref[n % 2, 0:8, :] = carry_ref[:, cc]
            carry_ref[:, cc] = u_ref[n % 2, tm:tm + 8, :]

    project(0)
    for n, c in enumerate(order):
        if n + 1 < len(order):
            project(n + 1)
        acc = u_ref[n % 2, 8:8 + tm, :]
        if c >= OFF_XS:
            cc = slice(c - OFF_XS, c - OFF_XS + nc)
            acc = cb_ref[:, cc] + cw_ref[SSD_CONV - 1:SSD_CONV, cc] * acc
            for k in range(1, SSD_CONV):
                acc = acc + cw_ref[SSD_CONV - 1 - k:SSD_CONV - k, cc] * u_ref[n % 2, 8 - k:8 - k + tm, :]
        if c >= OFF_Z:
            acc = acc * _sigmoid(acc)
        main_ref[:, c:c + nc] = acc.astype(BF16)
    small_ref[...] = _dot(h, ws_ref[...])


def _inproj(x2, g, wm_all, ws_all, layer, cw, cb, seq):
    t = x2.shape[0]
    tm = TM_PROJ
    nc = PROJ_CHUNK
    n_conv = N_MAIN - OFF_XS
    return pl.pallas_call(
        functools.partial(_inproj_kernel, tm=tm, nc=nc, tiles_per_seq=seq // tm),
        grid=(t // tm,),
        in_specs=[
            pl.BlockSpec((tm, D_MODEL), lambda i: (i, 0)),
            pl.BlockSpec((1, D_MODEL), lambda i: (0, 0)),
            _layer_weight(D_MODEL, N_MAIN, layer),
            _layer_weight(D_MODEL, LANES, layer),
            pl.BlockSpec((SSD_CONV, n_conv), lambda i: (0, 0)),
            pl.BlockSpec((1, n_conv), lambda i: (0, 0)),
        ],
        out_specs=[
            pl.BlockSpec((tm, N_MAIN), lambda i: (i, 0)),
            pl.BlockSpec((tm, LANES), lambda i: (i, 0)),
        ],
        out_shape=[
            jax.ShapeDtypeStruct((t, N_MAIN), BF16),
            jax.ShapeDtypeStruct((t, LANES), F32),
        ],
        scratch_shapes=[
            pltpu.VMEM((8, n_conv), F32),
            pltpu.VMEM((2, tm + 8, nc), F32),
        ],
        compiler_params=_cparams(("arbitrary",)),
        name="inproj",
    )(x2, g, wm_all, ws_all, cw, cb)


def _fox_gate_kernel(s_ref, bias_ref, c_ref, cp_ref, carry_ref, *, tb):
    @pl.when(pl.program_id(1) == 0)
    def _():
        carry_ref[...] = jnp.zeros_like(carry_ref)

    xx = s_ref[0] + bias_ref[...]
    log_f = jnp.minimum(xx, 0.0) - _softplus_neg_abs(xx)
    row = lax.broadcasted_iota(jnp.int32, (tb, tb), 0)
    col = lax.broadcasted_iota(jnp.int32, (tb, tb), 1)
    tri = jnp.where(row >= col, 1.0, 0.0).astype(BF16)
    cum = _mask_dot_left(tri, log_f, 3) + carry_ref[...]
    c_ref[0] = cum
    carry_ref[...] = cum[tb - 1:tb, :]
    src = lax.broadcasted_iota(jnp.int32, (LANES, LANES), 0) - SMALL_FF
    dst = lax.broadcasted_iota(jnp.int32, (LANES, LANES), 1)
    head_ok = (src >= 0) & (src < FOX_HEADS)
    cp = None
    for j, part in enumerate(_split_bf16(-cum, CP_TERMS)):
        sel = jnp.where(head_ok & (dst == CP_TERMS * src + j), 1.0, 0.0).astype(BF16)
        t = _dot(part, sel)
        cp = t if cp is None else cp + t
    cp_ref[0] = cp.astype(BF16)


def _fox_gate(small3, bias_row):
    b, s, _ = small3.shape
    tb = GATE_ROWS
    blk = pl.BlockSpec((1, tb, LANES), lambda bi, i: (bi, i, 0))
    return pl.pallas_call(
        functools.partial(_fox_gate_kernel, tb=tb),
        grid=(b, s // tb),
        in_specs=[blk, pl.BlockSpec((1, LANES), lambda bi, i: (0, 0))],
        out_specs=[blk, blk],
        out_shape=[jax.ShapeDtypeStruct((b, s, LANES), F32),
                   jax.ShapeDtypeStruct((b, s, LANES), BF16)],
        scratch_shapes=[pltpu.VMEM((1, LANES), F32)],
        compiler_params=_cparams(("arbitrary", "arbitrary")),
        name="fox_gate",
    )(small3, bias_row)


def _fox_kernel(q_ref, k_ref, v_ref, cp_ref, cq_ref, g_ref, o_ref, vh_ref, m_ref, acc_ref, c_ref,
                alpha_ref, t_ref, p_ref, *, tq, npairs):
    p_idx = pl.program_id(1)
    i = pl.program_id(2)
    lane = lax.broadcasted_iota(jnp.int32, (1, LANES), 1)
    lo_half = lane < HEAD_DIM
    nheads = 2 * npairs
    pair = lambda h: slice((h // 2) * LANES, (h // 2 + 1) * LANES)

    @pl.when(i == 0)
    def _():
        for pr in range(npairs):
            v2 = v_ref[0, :, pair(2 * pr)]
            one = jnp.ones_like(v2)
            vh_ref[2 * pr] = jnp.where(lo_half, v2, one)
            vh_ref[2 * pr + 1] = jnp.where(lo_half, one, v2)

    q_aug = []
    for h in range(nheads):
        q2 = q_ref[0, :, pair(h)]
        own = lo_half if h % 2 == 0 else jnp.logical_not(lo_half)
        first = CP_TERMS * (nheads * p_idx + h)
        ones_at = jnp.where((lane >= first) & (lane < first + CP_TERMS), 1.0, 0.0).astype(BF16)
        q_aug.append(jnp.concatenate(
            [jnp.where(own, q2, jnp.zeros_like(q2)) * ATTN_SCALE, jnp.broadcast_to(ones_at, (tq, LANES))],
            axis=1))
        c_lane = SMALL_FF + nheads * p_idx + h
        c_ref[h] = jnp.broadcast_to(
            jnp.sum(jnp.where(lane == c_lane, cq_ref[0], 0.0), axis=-1, keepdims=True), (tq, LANES))

    m_ref[...] = jnp.full_like(m_ref, -jnp.inf)
    acc_ref[...] = jnp.zeros_like(acc_ref)

    def tile(start, tk, masked):
        cp_t = cp_ref[0, pl.ds(start, tk), :]
        for h in range(nheads):
            k_aug = jnp.concatenate([k_ref[0, pl.ds(start, tk), pair(h)], cp_t], axis=1)
            t_ref[h, :, 0:tk] = _dot_nt(q_aug[h], k_aug)
        for h in range(nheads):
            for r in range(0, tq, ROW_BLOCK):
                rows = slice(r, r + ROW_BLOCK)
                t = t_ref[h, rows, 0:tk]
                if masked:
                    row = lax.broadcasted_iota(jnp.int32, (ROW_BLOCK, tk), 0) + r
                    col = lax.broadcasted_iota(jnp.int32, (ROW_BLOCK, tk), 1)
                    t = jnp.where(row >= col, t, -jnp.inf)
                m_prev = m_ref[h, rows, :]
                m_new = jnp.maximum(m_prev, jnp.max(t, axis=-1, keepdims=True) + c_ref[h, rows, :])
                shift = m_new - c_ref[h, rows, :]
                for c in range(0, tk, LANES):
                    p_ref[h, rows, c:c + LANES] = jnp.exp(t[:, c:c + LANES] - shift).astype(BF16)
                alpha_ref[h, rows, :] = jnp.exp(m_prev - m_new)
                m_ref[h, rows, :] = m_new
        for h in range(nheads):
            acc_ref[h] = alpha_ref[h] * acc_ref[h] + _dot(p_ref[h, :, 0:tk], vh_ref[h, pl.ds(start, tk), :])

    def body(j, carry):
        tile(pl.multiple_of(j * (2 * tq), 2 * tq), 2 * tq, False)
        return carry

    lax.fori_loop(0, i // 2, body, 0)

    @pl.when(i % 2 == 1)
    def _():
        tile(pl.multiple_of((i - 1) * tq, tq), tq, False)

    tile(pl.multiple_of(i * tq, tq), tq, True)

    for pr in range(npairs):
        a0, a1 = acc_ref[2 * pr], acc_ref[2 * pr + 1]
        num = jnp.where(lo_half, a0, a1)
        den = pltpu.roll(jnp.where(lo_half, a1, a0), HEAD_DIM, axis=1)
        o_ref[0, :, pair(2 * pr)] = _head_pair_norm(num / den, g_ref[:, pair(2 * pr)], lane).astype(BF16)


def _fox_attention(main3, cp, cq, gain):
    b, s, _ = main3.shape
    tq = TQ_FOX
    npairs = FOX_PAIRS_PER_STEP
    nheads = 2 * npairs
    w = npairs * LANES
    nsteps = D_FOX // w
    qb, kb, vb = OFF_FOX // w, (OFF_FOX + D_FOX) // w, (OFF_FOX + 2 * D_FOX) // w
    return pl.pallas_call(
        functools.partial(_fox_kernel, tq=tq, npairs=npairs),
        grid=(b, nsteps, s // tq),
        in_specs=[
            pl.BlockSpec((1, tq, w), lambda bi, p, i: (bi, i, qb + p)),
            pl.BlockSpec((1, s, w), lambda bi, p, i: (bi, 0, kb + p)),
            pl.BlockSpec((1, s, w), lambda bi, p, i: (bi, 0, vb + p)),
            pl.BlockSpec((1, s, LANES), lambda bi, p, i: (bi, 0, 0)),
            pl.BlockSpec((1, tq, LANES), lambda bi, p, i: (bi, i, 0)),
            pl.BlockSpec((1, w), lambda bi, p, i: (0, p)),
        ],
        out_specs=pl.BlockSpec((1, tq, w), lambda bi, p, i: (bi, i, p)),
        out_shape=jax.ShapeDtypeStruct((b, s, D_FOX), BF16),
        scratch_shapes=[
            pltpu.VMEM((nheads, s, LANES), BF16),
            pltpu.VMEM((nheads, tq, LANES), F32),
            pltpu.VMEM((nheads, tq, LANES), F32),
            pltpu.VMEM((nheads, tq, LANES), F32),
            pltpu.VMEM((nheads, tq, LANES), F32),
            pltpu.VMEM((nheads, tq, 2 * tq), F32),
            pltpu.VMEM((nheads, tq, 2 * tq), BF16),
        ],
        compiler_params=_cparams(("arbitrary", "arbitrary", "arbitrary")),
        name="fox_attn",
    )(main3, main3, main3, cp, cq, gain)


def _sb_kernel(q_ref, k_ref, v_ref, g_ref, o_ref, r_ref, acc_ref, z_ref, lb_ref, lk_ref, w_ref, *,
               tq, npairs):
    i = pl.program_id(2)
    lane = lax.broadcasted_iota(jnp.int32, (1, LANES), 1)
    nheads = 2 * npairs
    qh = []
    for pr in range(npairs):
        q2 = q_ref[0, :, pr * LANES:(pr + 1) * LANES]
        zero = jnp.zeros_like(q2)
        qh += [jnp.where(lane < HEAD_DIM, q2, zero) * ATTN_SCALE,
               jnp.where(lane < HEAD_DIM, zero, q2) * ATTN_SCALE]
    row = lax.broadcasted_iota(jnp.int32, (tq, tq), 0)
    col = lax.broadcasted_iota(jnp.int32, (tq, tq), 1)
    later = jnp.where(row > col, 1.0, 0.0).astype(BF16)

    r_ref[...] = jnp.zeros_like(r_ref)
    acc_ref[...] = jnp.zeros_like(acc_ref)

    def tile(j, masked):
        start = pl.multiple_of(j * tq, tq)
        pair = lambda hh: slice((hh // 2) * LANES, (hh // 2 + 1) * LANES)
        for hh in range(nheads):
            z_ref[hh] = _dot_nt(qh[hh], k_ref[0, pl.ds(start, tq), pair(hh)])
        r_prev = []
        for hh in range(nheads):
            z = z_ref[hh]
            log_beta = jnp.minimum(z, 0.0) - _softplus_neg_abs(z)
            log_keep = log_beta - z
            if masked:
                log_keep = jnp.where(col < row, log_keep, 0.0)
            lb_ref[hh] = log_beta
            lk_ref[hh] = log_keep.astype(BF16)
            r_prev.append(r_ref[hh])
            r_ref[hh] = r_prev[hh] + jnp.sum(log_keep, axis=-1, keepdims=True)
        for hh in range(nheads):
            z_ref[hh] = _dot(lk_ref[hh], later)
        for hh in range(nheads):
            log_w = lb_ref[hh] + z_ref[hh]
            chunks = []
            for c in range(0, tq, LANES):
                lw = log_w[:, c:c + LANES] + r_prev[hh]
                if masked:
                    row_c = lax.broadcasted_iota(jnp.int32, (tq, LANES), 0)
                    col_c = lax.broadcasted_iota(jnp.int32, (tq, LANES), 1) + c
                    lw = jnp.where(col_c < row_c, lw, -jnp.inf)
                chunks.append(jnp.exp(lw).astype(BF16))
            w_ref[hh] = jnp.concatenate(chunks, axis=1)
        for hh in range(nheads):
            acc_ref[hh] = acc_ref[hh] + _dot(w_ref[hh], v_ref[0, pl.ds(start, tq), pair(hh)])

    tile(i, True)

    def cond(carry):
        it, live = carry
        return jnp.logical_and(it < i, live)

    def body(carry):
        it, _ = carry
        tile(i - 1 - it, False)
        return it + 1, jnp.max(r_ref[...]) > EXP_UNDERFLOW

    lax.while_loop(cond, body, (0, True))

    for pr in range(npairs):
        ps = slice(pr * LANES, (pr + 1) * LANES)
        o2 = jnp.where(lane < HEAD_DIM, acc_ref[2 * pr], acc_ref[2 * pr + 1])
        o_ref[0, :, ps] = _head_pair_norm(o2, g_ref[:, ps], lane).astype(BF16)


def _sb_attention(main3, gain):
    b, s, _ = main3.shape
    tq = TQ
    npairs = SB_PAIRS_PER_STEP
    nheads = 2 * npairs
    w = npairs * LANES
    nsteps = D_SB // w
    qb, kb, vb = OFF_SB // w, (OFF_SB + D_SB) // w, (OFF_SB + 2 * D_SB) // w
    return pl.pallas_call(
        functools.partial(_sb_kernel, tq=tq, npairs=npairs),
        grid=(b, nsteps, s // tq),
        in_specs=[
            pl.BlockSpec((1, tq, w), lambda bi, p, i: (bi, i, qb + p)),
            pl.BlockSpec((1, s, w), lambda bi, p, i: (bi, 0, kb + p)),
            pl.BlockSpec((1, s, w), lambda bi, p, i: (bi, 0, vb + p)),
            pl.BlockSpec((1, w), lambda bi, p, i: (0, p)),
        ],
        out_specs=pl.BlockSpec((1, tq, w), lambda bi, p, i: (bi, i, p)),
        out_shape=jax.ShapeDtypeStruct((b, s, D_SB), BF16),
        scratch_shapes=[
            pltpu.VMEM((nheads, tq, LANES), F32),
            pltpu.VMEM((nheads, tq, LANES), F32),
            pltpu.VMEM((nheads, tq, tq), F32),
            pltpu.VMEM((nheads, tq, tq), F32),
            pltpu.VMEM((nheads, tq, tq), BF16),
            pltpu.VMEM((nheads, tq, tq), BF16),
        ],
        compiler_params=_cparams(("arbitrary", "arbitrary", "arbitrary")),
        name="sb_attn",
    )(main3, main3, main3, gain)


def _ssd_kernel(z_ref, xs_ref, bc_ref, dt_ref, dtb_ref, alog_ref, dexp_ref, ng_ref, e_ref, o_ref, state_ref):
    L = SSD_CHUNK
    hpg = SSD_HEADS // SSD_GROUPS
    gw = hpg * HEAD_DIM

    @pl.when(pl.program_id(1) == 0)
    def _():
        state_ref[...] = jnp.zeros_like(state_ref)

    def chunk(rows):
        xdt_raw = dt_ref[0, rows, :] + dtb_ref[...]
        dt = jnp.maximum(xdt_raw, 0.0) + _softplus_neg_abs(xdt_raw)
        a = -jnp.exp(alog_ref[...])
        da = dt * a
        row = lax.broadcasted_iota(jnp.int32, (L, L), 0)
        col = lax.broadcasted_iota(jnp.int32, (L, L), 1)
        causal = row >= col
        tri = jnp.where(causal, 1.0, 0.0).astype(BF16)
        a_cs = _mask_dot_left(tri, da, 3)
        a_cs_t = a_cs.T
        a_last = a_cs[L - 1:L, :]

        per_head = jnp.concatenate([dt, jnp.exp(a_cs), jnp.exp(a_last - a_cs),
                                    jnp.broadcast_to(jnp.exp(a_last), (EXPAND_PAD, LANES))], axis=0)
        per_head = jnp.concatenate(_split_bf16(per_head, 2), axis=1)

        lane = lax.broadcasted_iota(jnp.int32, (1, LANES), 1)
        for g in range(SSD_GROUPS):
            sl = slice(g * gw, (g + 1) * gw)
            wide = _dot(per_head, e_ref[:, sl])
            dt_x = wide[0:L]
            dec_in_x = wide[L:2 * L]
            dec_end_x = wide[2 * L:3 * L]
            chunk_dec_x = wide[3 * L:3 * L + 1]

            xs = xs_ref[0, rows, sl].astype(F32)
            xdt = xs * dt_x
            xdt_b = xdt.astype(BF16)
            xend_b = (xdt * dec_end_x).astype(BF16)

            b_g = bc_ref[0, rows, g * SSD_STATE:(g + 1) * SSD_STATE]
            c_g = bc_ref[0, rows, D_BC + g * SSD_STATE:D_BC + (g + 1) * SSD_STATE]
            cb = _dot_nt(c_g, b_g)
            st = state_ref[g]
            y_off = _dot(c_g, st.astype(BF16)) * dec_in_x
            y_diag = []
            for pr in range(hpg // 2):
                outs = []
                for hh in range(2):
                    h = g * hpg + 2 * pr + hh
                    seg = jnp.exp(jnp.where(causal, a_cs[:, h:h + 1] - a_cs_t[h:h + 1, :], -jnp.inf))
                    m = (cb * seg).astype(BF16)
                    outs.append(_dot(m, xdt_b[:, pr * LANES:(pr + 1) * LANES]))
                y_diag.append(jnp.where(lane < HEAD_DIM, outs[0], outs[1]))
            state_ref[g] = chunk_dec_x * st + _dot_tn(b_g, xend_b)

            y_g = jnp.concatenate(y_diag, axis=-1) + y_off
            y_g = (y_g + xs * dexp_ref[:, sl]) * z_ref[0, rows, sl].astype(F32)
            o_ref[0, rows, sl] = _rms(y_g, ng_ref[:, sl]).astype(BF16)

    for sub in range(SSD_CHUNKS_PER_STEP):
        chunk(slice(sub * L, (sub + 1) * L))


def _ssd(main3, small3, dtb, alog, dexp, ng, e01):
    b, s, _ = main3.shape
    L = SSD_CHUNK * SSD_CHUNKS_PER_STEP
    zb, xb, bcb = OFF_Z // D_SSD, OFF_XS // D_SSD, OFF_BC // (2 * D_BC)
    const = lambda shape: pl.BlockSpec(shape, lambda bi, c: (0,) * len(shape))
    return pl.pallas_call(
        _ssd_kernel,
        grid=(b, s // L),
        in_specs=[
            pl.BlockSpec((1, L, D_SSD), lambda bi, c: (bi, c, zb)),
            pl.BlockSpec((1, L, D_SSD), lambda bi, c: (bi, c, xb)),
            pl.BlockSpec((1, L, 2 * D_BC), lambda bi, c: (bi, c, bcb)),
            pl.BlockSpec((1, L, LANES), lambda bi, c: (bi, c, 0)),
            const((1, LANES)), const((1, LANES)),
            const((1, D_SSD)), const((1, D_SSD)),
            const((2 * LANES, D_SSD)),
        ],
        out_specs=pl.BlockSpec((1, L, D_SSD), lambda bi, c: (bi, c, 0)),
        out_shape=jax.ShapeDtypeStruct((b, s, D_SSD), BF16),
        scratch_shapes=[pltpu.VMEM((SSD_GROUPS, SSD_STATE, D_SSD // SSD_GROUPS), F32)],
        compiler_params=_cparams(("arbitrary", "arbitrary")),
        name="ssd",
    )(main3, main3, main3, small3, dtb, alog, dexp, ng, e01)


def _ffn_kernel(x_ref, yf_ref, ys_ref, yd_ref, wo_ref, g_ref, wu_ref, cw_ref, cb_ref, wd_ref, fg_ref, o_ref,
                carry_ref, u_ref, act_ref, *, tm, tiles_per_seq, final_norm):
    fc = FFN_FC
    nchunk = D_FF // fc

    @pl.when(pl.program_id(0) % tiles_per_seq == 0)
    def _():
        carry_ref[...] = jnp.zeros_like(carry_ref)

    x = (x_ref[...] + _dot(yf_ref[...], wo_ref[0:D_FOX, :])
         + _dot(ys_ref[...], wo_ref[D_FOX:D_FOX + D_SB, :])
         + _dot(yd_ref[...], wo_ref[D_FOX + D_SB:D_FOX + D_SB + D_SSD, :]))
    h = _rms(x, g_ref[...]).astype(BF16)

    def cols(c):
        return slice(c * fc, (c + 1) * fc), slice(D_FF + c * fc, D_FF + (c + 1) * fc)

    def up(c):
        for half, cs in enumerate(cols(c)):
            hs = slice(half * fc, (half + 1) * fc)
            u_ref[c % 2, 0:8, hs] = carry_ref[:, cs]
            u_ref[c % 2, 8:8 + tm, hs] = _dot(h, wu_ref[:, cs])
            carry_ref[:, cs] = u_ref[c % 2, tm:tm + 8, hs]

    up(0)
    for c in range(nchunk):
        if c + 1 < nchunk:
            up(c + 1)
        ys = []
        for half, cs in enumerate(cols(c)):
            hs = slice(half * fc, (half + 1) * fc)
            y = cb_ref[:, cs] + cw_ref[FFN_CONV - 1:FFN_CONV, cs] * u_ref[c % 2, 8:8 + tm, hs]
            for k in range(1, FFN_CONV):
                y = y + cw_ref[FFN_CONV - 1 - k:FFN_CONV - k, cs] * u_ref[c % 2, 8 - k:8 - k + tm, hs]
            ys.append(y)
        gate, val = ys
        act_ref[:, c * fc:(c + 1) * fc] = (gate * _sigmoid(gate) * val).astype(BF16)
    out = x + _dot(act_ref[...], wd_ref[...])
    if final_norm:
        out = _rms(out, fg_ref[...])
    o_ref[...] = out


def _ffn(x2, yf, ys, yd, wo_all, g, wu_all, cw, cb, wd_all, fg, layer, seq, final_norm):
    t = x2.shape[0]
    tm = TM_PROJ
    tok = lambda n: pl.BlockSpec((tm, n), lambda i: (i, 0))
    row = lambda n: pl.BlockSpec((1, n), lambda i: (0, 0))
    return pl.pallas_call(
        functools.partial(_ffn_kernel, tm=tm, tiles_per_seq=seq // tm, final_norm=final_norm),
        grid=(t // tm,),
        in_specs=[
            tok(D_MODEL), tok(D_FOX), tok(D_SB), tok(D_SSD),
            _layer_weight(D_FOX + D_SB + D_SSD, D_MODEL, layer),
            row(D_MODEL),
            _layer_weight(D_MODEL, 2 * D_FF, layer),
            pl.BlockSpec((FFN_CONV, 2 * D_FF), lambda i: (0, 0)),
            row(2 * D_FF),
            _layer_weight(D_FF, D_MODEL, layer),
            row(D_MODEL),
        ],
        out_specs=tok(D_MODEL),
        out_shape=jax.ShapeDtypeStruct((t, D_MODEL), F32),
        scratch_shapes=[
            pltpu.VMEM((8, 2 * D_FF), F32),
            pltpu.VMEM((2, tm + 8, 2 * FFN_FC), F32),
            pltpu.VMEM((tm, D_FF), BF16),
        ],
        compiler_params=_cparams(("arbitrary",)),
        name="ffn",
    )(x2, yf, ys, yd, wo_all, g, wu_all, cw, cb, wd_all, fg)


def _pad_lanes(v, offset):
    return jnp.zeros((1, LANES), F32).at[0, offset:offset + v.shape[0]].set(v.astype(F32))


def _layer(x2, b, s, layer, wm_all, ws_all, wo_all, wu_all, wd_all, mix_g, fox_f_bias, fox_out_g, sb_out_g,
           ssd_conv_w, ssd_conv_b, ssd_dt_bias, ssd_a_log, ssd_d, ssd_norm_g, ffn_g, ffn_conv_w, ffn_conv_b,
           final_g, final_norm):
    main2, small2 = _inproj(x2, mix_g[None, :], wm_all, ws_all, layer, ssd_conv_w, ssd_conv_b[None, :], s)
    main3 = main2.reshape(b, s, N_MAIN)
    small3 = small2.reshape(b, s, LANES)

    c_full, cp = _fox_gate(small3, _pad_lanes(fox_f_bias, SMALL_FF))
    y_fox = _fox_attention(main3, cp, c_full, fox_out_g[None, :])
    y_sb = _sb_attention(main3, sb_out_g[None, :])

    e01 = (jnp.arange(2 * LANES)[:, None] % LANES == (jnp.arange(D_SSD)[None, :] // HEAD_DIM)).astype(BF16)
    y_ssd = _ssd(main3, small3, _pad_lanes(ssd_dt_bias, SMALL_DT), _pad_lanes(ssd_a_log, SMALL_DT),
                 jnp.repeat(ssd_d, HEAD_DIM)[None, :], ssd_norm_g[None, :], e01)

    t = b * s
    return _ffn(x2, y_fox.reshape(t, D_FOX), y_sb.reshape(t, D_SB), y_ssd.reshape(t, D_SSD), wo_all,
                ffn_g[None, :], wu_all, ffn_conv_w, ffn_conv_b[None, :], wd_all, final_g[None, :],
                layer, s, final_norm)


def kernel(x, mix_norm_g, w_in, fox_f_bias, fox_out_g, sb_out_g, ssd_conv_w, ssd_conv_b, ssd_dt_bias,
           ssd_a_log, ssd_d, ssd_norm_g, w_out, ffn_norm_g, w_up, ffn_conv_w, ffn_conv_b, w_down,
           final_norm_g):
    b, s, d = x.shape
    depth = w_in.shape[0]
    ff0 = 3 * D_FOX
    dt0 = N_MAIN + FOX_HEADS
    wm_all = jnp.concatenate([w_in[:, :, :ff0], w_in[:, :, ff0 + FOX_HEADS:dt0]], axis=2).astype(BF16)
    ws_all = jnp.concatenate([w_in[:, :, dt0:dt0 + SSD_HEADS], w_in[:, :, ff0:ff0 + FOX_HEADS],
                              jnp.zeros((depth, D_MODEL, LANES - SSD_HEADS - FOX_HEADS), F32)],
                             axis=2).astype(BF16)
    wo_all, wu_all, wd_all = w_out.astype(BF16), w_up.astype(BF16), w_down.astype(BF16)
    x2 = x.reshape(b * s, d)
    for l in range(depth):
        x2 = _layer(x2, b, s, l, wm_all, ws_all, wo_all, wu_all, wd_all, mix_norm_g[l], fox_f_bias[l],
                    fox_out_g[l], sb_out_g[l], ssd_conv_w[l], ssd_conv_b[l], ssd_dt_bias[l], ssd_a_log[l],
                    ssd_d[l], ssd_norm_g[l], ffn_norm_g[l], ffn_conv_w[l], ffn_conv_b[l], final_norm_g,
                    l == depth - 1)
    return x2.reshape(b, s, d)
```

```python
import functools

import jax
import jax.numpy as jnp
from jax import lax
from jax.experimental import pallas as pl
from jax.experimental.pallas import tpu as pltpu

F32 = jnp.float32
BF16 = jnp.bfloat16

D_MODEL = 1024
HEAD_DIM = 64
FOX_HEADS = 8
SB_HEADS = 8
SSD_HEADS = 16
SSD_GROUPS = 2
SSD_STATE = 128
SSD_CONV = 4
SSD_CHUNK = 128
D_FOX = FOX_HEADS * HEAD_DIM
D_SB = SB_HEADS * HEAD_DIM
D_SSD = SSD_HEADS * HEAD_DIM
D_BC = SSD_GROUPS * SSD_STATE
D_FF = 2816
FFN_CONV = 3
NORM_EPS = 1e-6
CP_TERMS = 3
EXP_UNDERFLOW = -104.0

LANES = 128
N_MAIN = 3 * D_FOX + 3 * D_SB + D_SSD + D_SSD + 2 * D_BC
OFF_FOX = 0
OFF_SB = 3 * D_FOX
OFF_Z = OFF_SB + 3 * D_SB
OFF_XS = OFF_Z + D_SSD
OFF_BC = OFF_XS + D_SSD
SMALL_DT = 0
SMALL_FF = SSD_HEADS

VMEM_LIMIT = 56 * 1024 * 1024

ATTN_SCALE = HEAD_DIM ** -0.5
TM_PROJ = 512
PROJ_CHUNK = 512
GATE_ROWS = 512
TQ = 256
TQ_FOX = 512
SSD_CHUNKS_PER_STEP = 8
EXPAND_PAD = 16
ROW_BLOCK = 32
FOX_PAIRS_PER_STEP = 2
SB_PAIRS_PER_STEP = 4
FFN_FC = 256


def _cparams(sem):
    return pltpu.CompilerParams(dimension_semantics=sem, vmem_limit_bytes=VMEM_LIMIT)


def _layer_weight(rows, cols, layer):
    return pl.BlockSpec((None, rows, cols), lambda i: (layer, 0, 0), pipeline_mode=pl.Buffered(1))


def _split_bf16(x, n):
    parts, r = [], x
    for _ in range(n):
        p = r.astype(BF16)
        parts.append(p)
        r = r - p.astype(F32)
    return parts


def _dot(a, b):
    return jnp.dot(a, b, preferred_element_type=F32)


def _dot_nt(a, b):
    return lax.dot_general(a, b, (((1,), (1,)), ((), ())), preferred_element_type=F32)


def _dot_tn(a, b):
    return lax.dot_general(a, b, (((0,), (0,)), ((), ())), preferred_element_type=F32)


def _mask_dot_left(m01, x, n):
    out = None
    for p in _split_bf16(x, n):
        t = _dot(m01, p)
        out = t if out is None else out + t
    return out


def _softplus_neg_abs(x):
    return jnp.log(1.0 + jnp.exp(-jnp.abs(x)))


def _sigmoid(x):
    return 1.0 / (1.0 + jnp.exp(-x))


def _rms(x, g):
    ms = jnp.mean(x * x, axis=-1, keepdims=True)
    return x * lax.rsqrt(ms + NORM_EPS) * g


def _head_pair_norm(o2, g, lane):
    lo = lane < HEAD_DIM
    sq = o2 * o2
    ms0 = jnp.sum(jnp.where(lo, sq, 0.0), axis=-1, keepdims=True) * (1.0 / HEAD_DIM)
    ms1 = jnp.sum(jnp.where(lo, 0.0, sq), axis=-1, keepdims=True) * (1.0 / HEAD_DIM)
    ms = jnp.where(lo, ms0, ms1)
    return o2 * lax.rsqrt(ms + NORM_EPS) * g


def _inproj_kernel(x_ref, g_ref, wm_ref, ws_ref, cw_ref, cb_ref, main_ref, small_ref, carry_ref, u_ref,
                   *, tm, nc, tiles_per_seq):
    @pl.when(pl.program_id(0) % tiles_per_seq == 0)
    def _():
        carry_ref[...] = jnp.zeros_like(carry_ref)

    h = _rms(x_ref[...], g_ref[...]).astype(BF16)
    order = list(reversed(range(0, N_MAIN, nc)))

    def project(n):
        c = order[n]
        u_ref[n % 2, 8:8 + tm, :] = _dot(h, wm_ref[:, c:c + nc])
        if c >= OFF_XS:
            cc = slice(c - OFF_XS, c - OFF_XS + nc)
            u_ref[n % 2, 0:8, :] = carry_ref[:, cc]
            carry_ref[:, cc] = u_ref[n % 2, tm:tm + 8, :]

    project(0)
    for n, c in enumerate(order):
        if n + 1 < len(order):
            project(n + 1)
        acc = u_ref[n % 2, 8:8 + tm, :]
        if c >= OFF_XS:
            cc = slice(c - OFF_XS, c - OFF_XS + nc)
            acc = cb_ref[:, cc] + cw_ref[SSD_CONV - 1:SSD_CONV, cc] * acc
            for k in range(1, SSD_CONV):
                acc = acc + cw_ref[SSD_CONV - 1 - k:SSD_CONV - k, cc] * u_ref[n % 2, 8 - k:8 - k + tm, :]
        if c >= OFF_Z:
            acc = acc * _sigmoid(acc)
        main_ref[:, c:c + nc] = acc.astype(BF16)
    small_ref[...] = _dot(h, ws_ref[...])


def _inproj(x2, g, wm_all, ws_all, layer, cw, cb, seq):
    t = x2.shape[0]
    tm = TM_PROJ
    nc = PROJ_CHUNK
    n_conv = N_MAIN - OFF_XS
    return pl.pallas_call(
        functools.partial(_inproj_kernel, tm=tm, nc=nc, tiles_per_seq=seq // tm),
        grid=(t // tm,),
        in_specs=[
            pl.BlockSpec((tm, D_MODEL), lambda i: (i, 0)),
            pl.BlockSpec((1, D_MODEL), lambda i: (0, 0)),
            _layer_weight(D_MODEL, N_MAIN, layer),
            _layer_weight(D_MODEL, LANES, layer),
            pl.BlockSpec((SSD_CONV, n_conv), lambda i: (0, 0)),
            pl.BlockSpec((1, n_conv), lambda i: (0, 0)),
        ],
        out_specs=[
            pl.BlockSpec((tm, N_MAIN), lambda i: (i, 0)),
            pl.BlockSpec((tm, LANES), lambda i: (i, 0)),
        ],
        out_shape=[
            jax.ShapeDtypeStruct((t, N_MAIN), BF16),
            jax.ShapeDtypeStruct((t, LANES), F32),
        ],
        scratch_shapes=[
            pltpu.VMEM((8, n_conv), F32),
            pltpu.VMEM((2, tm + 8, nc), F32),
        ],
        compiler_params=_cparams(("arbitrary",)),
        name="inproj",
    )(x2, g, wm_all, ws_all, cw, cb)


def _fox_gate_kernel(s_ref, bias_ref, c_ref, cp_ref, carry_ref, *, tb):
    @pl.when(pl.program_id(1) == 0)
    def _():
        carry_ref[...] = jnp.zeros_like(carry_ref)

    xx = s_ref[0] + bias_ref[...]
    log_f = jnp.minimum(xx, 0.0) - _softplus_neg_abs(xx)
    row = lax.broadcasted_iota(jnp.int32, (tb, tb), 0)
    col = lax.broadcasted_iota(jnp.int32, (tb, tb), 1)
    tri = jnp.where(row >= col, 1.0, 0.0).astype(BF16)
    cum = _mask_dot_left(tri, log_f, 3) + carry_ref[...]
    c_ref[0] = cum
    carry_ref[...] = cum[tb - 1:tb, :]
    src = lax.broadcasted_iota(jnp.int32, (LANES, LANES), 0) - SMALL_FF
    dst = lax.broadcasted_iota(jnp.int32, (LANES, LANES), 1)
    head_ok = (src >= 0) & (src < FOX_HEADS)
    cp = None
    for j, part in enumerate(_split_bf16(-cum, CP_TERMS)):
        sel = jnp.where(head_ok & (dst == CP_TERMS * src + j), 1.0, 0.0).astype(BF16)
        t = _dot(part, sel)
        cp = t if cp is None else cp + t
    cp_ref[0] = cp.astype(BF16)


def _fox_gate(small3, bias_row):
    b, s, _ = small3.shape
    tb = GATE_ROWS
    blk = pl.BlockSpec((1, tb, LANES), lambda bi, i: (bi, i, 0))
    return pl.pallas_call(
        functools.partial(_fox_gate_kernel, tb=tb),
        grid=(b, s // tb),
        in_specs=[blk, pl.BlockSpec((1, LANES), lambda bi, i: (0, 0))],
        out_specs=[blk, blk],
        out_shape=[jax.ShapeDtypeStruct((b, s, LANES), F32),
                   jax.ShapeDtypeStruct((b, s, LANES), BF16)],
        scratch_shapes=[pltpu.VMEM((1, LANES), F32)],
        compiler_params=_cparams(("arbitrary", "arbitrary")),
        name="fox_gate",
    )(small3, bias_row)


def _fox_kernel(q_ref, k_ref, v_ref, cp_ref, cq_ref, g_ref, o_ref, vh_ref, m_ref, acc_ref, c_ref,
                alpha_ref, t_ref, p_ref, *, tq, npairs):
    p_idx = pl.program_id(1)
    i = pl.program_id(2)
    lane = lax.broadcasted_iota(jnp.int32, (1, LANES), 1)
    lo_half = lane < HEAD_DIM
    nheads = 2 * npairs
    pair = lambda h: slice((h // 2) * LANES, (h // 2 + 1) * LANES)

    @pl.when(i == 0)
    def _():
        for pr in range(npairs):
            v2 = v_ref[0, :, pair(2 * pr)]
            one = jnp.ones_like(v2)
            vh_ref[2 * pr] = jnp.where(lo_half, v2, one)
            vh_ref[2 * pr + 1] = jnp.where(lo_half, one, v2)

    q_aug = []
    for h in range(nheads):
        q2 = q_ref[0, :, pair(h)]
        own = lo_half if h % 2 == 0 else jnp.logical_not(lo_half)
        first = CP_TERMS * (nheads * p_idx + h)
        ones_at = jnp.where((lane >= first) & (lane < first + CP_TERMS), 1.0, 0.0).astype(BF16)
        q_aug.append(jnp.concatenate(
            [jnp.where(own, q2, jnp.zeros_like(q2)) * ATTN_SCALE, jnp.broadcast_to(ones_at, (tq, LANES))],
            axis=1))
        c_lane = SMALL_FF + nheads * p_idx + h
        c_ref[h] = jnp.broadcast_to(
            jnp.sum(jnp.where(lane == c_lane, cq_ref[0], 0.0), axis=-1, keepdims=True), (tq, LANES))

    m_ref[...] = jnp.full_like(m_ref, -jnp.inf)
    acc_ref[...] = jnp.zeros_like(acc_ref)

    def tile(start, tk, masked):
        cp_t = cp_ref[0, pl.ds(start, tk), :]
        for h in range(nheads):
            k_aug = jnp.concatenate([k_ref[0, pl.ds(start, tk), pair(h)], cp_t], axis=1)
            t_ref[h, :, 0:tk] = _dot_nt(q_aug[h], k_aug)
        for h in range(nheads):
            for r in range(0, tq, ROW_BLOCK):
                rows = slice(r, r + ROW_BLOCK)
                t = t_ref[h, rows, 0:tk]
                if masked:
                    row = lax.broadcasted_iota(jnp.int32, (ROW_BLOCK, tk), 0) + r
                    col = lax.broadcasted_iota(jnp.int32, (ROW_BLOCK, tk), 1)
                    t = jnp.where(row >= col, t, -jnp.inf)
                m_prev = m_ref[h, rows, :]
                m_new = jnp.maximum(m_prev, jnp.max(t, axis=-1, keepdims=True) + c_ref[h, rows, :])
                shift = m_new - c_ref[h, rows, :]
                for c in range(0, tk, LANES):
                    p_ref[h, rows, c:c + LANES] = jnp.exp((t[:, c:c + LANES] - shift).astype(BF16))
                alpha_ref[h, rows, :] = jnp.exp(m_prev - m_new)
                m_ref[h, rows, :] = m_new
        for h in range(nheads):
            acc_ref[h] = alpha_ref[h] * acc_ref[h] + _dot(p_ref[h, :, 0:tk], vh_ref[h, pl.ds(start, tk), :])

    def body(j, carry):
        tile(pl.multiple_of(j * (2 * tq), 2 * tq), 2 * tq, False)
        return carry

    lax.fori_loop(0, i // 2, body, 0)

    @pl.when(i % 2 == 1)
    def _():
        tile(pl.multiple_of((i - 1) * tq, tq), tq, False)

    tile(pl.multiple_of(i * tq, tq), tq, True)

    for pr in range(npairs):
        a0, a1 = acc_ref[2 * pr], acc_ref[2 * pr + 1]
        num = jnp.where(lo_half, a0, a1)
        den = pltpu.roll(jnp.where(lo_half, a1, a0), HEAD_DIM, axis=1)
        o_ref[0, :, pair(2 * pr)] = _head_pair_norm(num / den, g_ref[:, pair(2 * pr)], lane).astype(BF16)


def _fox_attention(main3, cp, cq, gain):
    b, s, _ = main3.shape
    tq = TQ_FOX
    npairs = FOX_PAIRS_PER_STEP
    nheads = 2 * npairs
    w = npairs * LANES
    nsteps = D_FOX // w
    qb, kb, vb = OFF_FOX // w, (OFF_FOX + D_FOX) // w, (OFF_FOX + 2 * D_FOX) // w
    return pl.pallas_call(
        functools.partial(_fox_kernel, tq=tq, npairs=npairs),
        grid=(b, nsteps, s // tq),
        in_specs=[
            pl.BlockSpec((1, tq, w), lambda bi, p, i: (bi, i, qb + p)),
            pl.BlockSpec((1, s, w), lambda bi, p, i: (bi, 0, kb + p)),
            pl.BlockSpec((1, s, w), lambda bi, p, i: (bi, 0, vb + p)),
            pl.BlockSpec((1, s, LANES), lambda bi, p, i: (bi, 0, 0)),
            pl.BlockSpec((1, tq, LANES), lambda bi, p, i: (bi, i, 0)),
            pl.BlockSpec((1, w), lambda bi, p, i: (0, p)),
        ],
        out_specs=pl.BlockSpec((1, tq, w), lambda bi, p, i: (bi, i, p)),
        out_shape=jax.ShapeDtypeStruct((b, s, D_FOX), BF16),
        scratch_shapes=[
            pltpu.VMEM((nheads, s, LANES), BF16),
            pltpu.VMEM((nheads, tq, LANES), F32),
            pltpu.VMEM((nheads, tq, LANES), F32),
            pltpu.VMEM((nheads, tq, LANES), F32),
            pltpu.VMEM((nheads, tq, LANES), F32),
            pltpu.VMEM((nheads, tq, 2 * tq), F32),
            pltpu.VMEM((nheads, tq, 2 * tq), BF16),
        ],
        compiler_params=_cparams(("arbitrary", "arbitrary", "arbitrary")),
        name="fox_attn",
    )(main3, main3, main3, cp, cq, gain)


def _sb_kernel(q_ref, k_ref, v_ref, g_ref, o_ref, r_ref, acc_ref, z_ref, lb_ref, lk_ref, w_ref, *,
               tq, npairs):
    i = pl.program_id(2)
    lane = lax.broadcasted_iota(jnp.int32, (1, LANES), 1)
    nheads = 2 * npairs
    qh = []
    for pr in range(npairs):
        q2 = q_ref[0, :, pr * LANES:(pr + 1) * LANES]
        zero = jnp.zeros_like(q2)
        qh += [jnp.where(lane < HEAD_DIM, q2, zero) * ATTN_SCALE,
               jnp.where(lane < HEAD_DIM, zero, q2) * ATTN_SCALE]
    row = lax.broadcasted_iota(jnp.int32, (tq, tq), 0)
    col = lax.broadcasted_iota(jnp.int32, (tq, tq), 1)
    later = jnp.where(row > col, 1.0, 0.0).astype(BF16)

    r_ref[...] = jnp.zeros_like(r_ref)
    acc_ref[...] = jnp.zeros_like(acc_ref)

    def tile(j, masked):
        start = pl.multiple_of(j * tq, tq)
        pair = lambda hh: slice((hh // 2) * LANES, (hh // 2 + 1) * LANES)
        for hh in range(nheads):
            z_ref[hh] = _dot_nt(qh[hh], k_ref[0, pl.ds(start, tq), pair(hh)])
        r_prev = []
        for hh in range(nheads):
            z = z_ref[hh]
            log_beta = jnp.minimum(z, 0.0) - _softplus_neg_abs(z)
            log_keep = log_beta - z
            if masked:
                log_keep = jnp.where(col < row, log_keep, 0.0)
            lb_ref[hh] = log_beta
            lk_ref[hh] = log_keep.astype(BF16)
            r_prev.append(r_ref[hh])
            r_ref[hh] = r_prev[hh] + jnp.sum(log_keep, axis=-1, keepdims=True)
        for hh in range(nheads):
            z_ref[hh] = _dot(lk_ref[hh], later)
        for hh in range(nheads):
            log_w = lb_ref[hh] + z_ref[hh]
            chunks = []
            for c in range(0, tq, LANES):
                lw = log_w[:, c:c + LANES] + r_prev[hh]
                if masked:
                    row_c = lax.broadcasted_iota(jnp.int32, (tq, LANES), 0)
                    col_c = lax.broadcasted_iota(jnp.int32, (tq, LANES), 1) + c
                    lw = jnp.where(col_c < row_c, lw, -jnp.inf)
                chunks.append(jnp.exp(lw).astype(BF16))
            w_ref[hh] = jnp.concatenate(chunks, axis=1)
        for hh in range(nheads):
            acc_ref[hh] = acc_ref[hh] + _dot(w_ref[hh], v_ref[0, pl.ds(start, tq), pair(hh)])

    tile(i, True)

    def cond(carry):
        it, live = carry
        return jnp.logical_and(it < i, live)

    def body(carry):
        it, _ = carry
        tile(i - 1 - it, False)
        return it + 1, jnp.max(r_ref[...]) > EXP_UNDERFLOW

    lax.while_loop(cond, body, (0, True))

    for pr in range(npairs):
        ps = slice(pr * LANES, (pr + 1) * LANES)
        o2 = jnp.where(lane < HEAD_DIM, acc_ref[2 * pr], acc_ref[2 * pr + 1])
        o_ref[0, :, ps] = _head_pair_norm(o2, g_ref[:, ps], lane).astype(BF16)


def _sb_attention(main3, gain):
    b, s, _ = main3.shape
    tq = TQ
    npairs = SB_PAIRS_PER_STEP
    nheads = 2 * npairs
    w = npairs * LANES
    nsteps = D_SB // w
    qb, kb, vb = OFF_SB // w, (OFF_SB + D_SB) // w, (OFF_SB + 2 * D_SB) // w
    return pl.pallas_call(
        functools.partial(_sb_kernel, tq=tq, npairs=npairs),
        grid=(b, nsteps, s // tq),
        in_specs=[
            pl.BlockSpec((1, tq, w), lambda bi, p, i: (bi, i, qb + p)),
            pl.BlockSpec((1, s, w), lambda bi, p, i: (bi, 0, kb + p)),
            pl.BlockSpec((1, s, w), lambda bi, p, i: (bi, 0, vb + p)),
            pl.BlockSpec((1, w), lambda bi, p, i: (0, p)),
        ],
        out_specs=pl.BlockSpec((1, tq, w), lambda bi, p, i: (bi, i, p)),
        out_shape=jax.ShapeDtypeStruct((b, s, D_SB), BF16),
        scratch_shapes=[
            pltpu.VMEM((nheads, tq, LANES), F32),
            pltpu.VMEM((nheads, tq, LANES), F32),
            pltpu.VMEM((nheads, tq, tq), F32),
            pltpu.VMEM((nheads, tq, tq), F32),
            pltpu.VMEM((nheads, tq, tq), BF16),
            pltpu.VMEM((nheads, tq, tq), BF16),
        ],
        compiler_params=_cparams(("arbitrary", "arbitrary", "arbitrary")),
        name="sb_attn",
    )(main3, main3, main3, gain)


def _ssd_kernel(z_ref, xs_ref, bc_ref, dt_ref, dtb_ref, alog_ref, dexp_ref, ng_ref, e_ref, o_ref, state_ref):
    L = SSD_CHUNK
    hpg = SSD_HEADS // SSD_GROUPS
    gw = hpg * HEAD_DIM

    @pl.when(pl.program_id(1) == 0)
    def _():
        state_ref[...] = jnp.zeros_like(state_ref)

    def chunk(rows):
        xdt_raw = dt_ref[0, rows, :] + dtb_ref[...]
        dt = jnp.maximum(xdt_raw, 0.0) + _softplus_neg_abs(xdt_raw)
        a = -jnp.exp(alog_ref[...])
        da = dt * a
        row = lax.broadcasted_iota(jnp.int32, (L, L), 0)
        col = lax.broadcasted_iota(jnp.int32, (L, L), 1)
        causal = row >= col
        tri = jnp.where(causal, 1.0, 0.0).astype(BF16)
        a_cs = _mask_dot_left(tri, da, 3)
        a_cs_t = a_cs.T
        a_last = a_cs[L - 1:L, :]

        per_head = jnp.concatenate([dt, jnp.exp(a_cs), jnp.exp(a_last - a_cs),
                                    jnp.broadcast_to(jnp.exp(a_last), (EXPAND_PAD, LANES))], axis=0)
        per_head = jnp.concatenate(_split_bf16(per_head, 2), axis=1)

        lane = lax.broadcasted_iota(jnp.int32, (1, LANES), 1)
        for g in range(SSD_GROUPS):
            sl = slice(g * gw, (g + 1) * gw)
            wide = _dot(per_head, e_ref[:, sl])
            dt_x = wide[0:L]
            dec_in_x = wide[L:2 * L]
            dec_end_x = wide[2 * L:3 * L]
            chunk_dec_x = wide[3 * L:3 * L + 1]

            xs = xs_ref[0, rows, sl].astype(F32)
            xdt = xs * dt_x
            xdt_b = xdt.astype(BF16)
            xend_b = (xdt * dec_end_x).astype(BF16)

            b_g = bc_ref[0, rows, g * SSD_STATE:(g + 1) * SSD_STATE]
            c_g = bc_ref[0, rows, D_BC + g * SSD_STATE:D_BC + (g + 1) * SSD_STATE]
            cb = _dot_nt(c_g, b_g)
            st = state_ref[g]
            y_off = _dot(c_g, st.astype(BF16)) * dec_in_x
            y_diag = []
            for pr in range(hpg // 2):
                outs = []
                for hh in range(2):
                    h = g * hpg + 2 * pr + hh
                    seg = jnp.exp(jnp.where(causal, a_cs[:, h:h + 1] - a_cs_t[h:h + 1, :], -jnp.inf))
                    m = (cb * seg).astype(BF16)
                    outs.append(_dot(m, xdt_b[:, pr * LANES:(pr + 1) * LANES]))
                y_diag.append(jnp.where(lane < HEAD_DIM, outs[0], outs[1]))
            state_ref[g] = chunk_dec_x * st + _dot_tn(b_g, xend_b)

            y_g = jnp.concatenate(y_diag, axis=-1) + y_off
            y_g = (y_g + xs * dexp_ref[:, sl]) * z_ref[0, rows, sl].astype(F32)
            o_ref[0, rows, sl] = _rms(y_g, ng_ref[:, sl]).astype(BF16)

    for sub in range(SSD_CHUNKS_PER_STEP):
        chunk(slice(sub * L, (sub + 1) * L))


def _ssd(main3, small3, dtb, alog, dexp, ng, e01):
    b, s, _ = main3.shape
    L = SSD_CHUNK * SSD_CHUNKS_PER_STEP
    zb, xb, bcb = OFF_Z // D_SSD, OFF_XS // D_SSD, OFF_BC // (2 * D_BC)
    const = lambda shape: pl.BlockSpec(shape, lambda bi, c: (0,) * len(shape))
    return pl.pallas_call(
        _ssd_kernel,
        grid=(b, s // L),
        in_specs=[
            pl.BlockSpec((1, L, D_SSD), lambda bi, c: (bi, c, zb)),
            pl.BlockSpec((1, L, D_SSD), lambda bi, c: (bi, c, xb)),
            pl.BlockSpec((1, L, 2 * D_BC), lambda bi, c: (bi, c, bcb)),
            pl.BlockSpec((1, L, LANES), lambda bi, c: (bi, c, 0)),
            const((1, LANES)), const((1, LANES)),
            const((1, D_SSD)), const((1, D_SSD)),
            const((2 * LANES, D_SSD)),
        ],
        out_specs=pl.BlockSpec((1, L, D_SSD), lambda bi, c: (bi, c, 0)),
        out_shape=jax.ShapeDtypeStruct((b, s, D_SSD), BF16),
        scratch_shapes=[pltpu.VMEM((SSD_GROUPS, SSD_STATE, D_SSD // SSD_GROUPS), F32)],
        compiler_params=_cparams(("arbitrary", "arbitrary")),
        name="ssd",
    )(main3, main3, main3, small3, dtb, alog, dexp, ng, e01)


def _ffn_kernel(x_ref, yf_ref, ys_ref, yd_ref, wo_ref, g_ref, wu_ref, cw_ref, cb_ref, wd_ref, fg_ref, o_ref,
                carry_ref, u_ref, act_ref, *, tm, tiles_per_seq, final_norm):
    fc = FFN_FC
    nchunk = D_FF // fc

    @pl.when(pl.program_id(0) % tiles_per_seq == 0)
    def _():
        carry_ref[...] = jnp.zeros_like(carry_ref)

    x = (x_ref[...] + _dot(yf_ref[...], wo_ref[0:D_FOX, :])
         + _dot(ys_ref[...], wo_ref[D_FOX:D_FOX + D_SB, :])
         + _dot(yd_ref[...], wo_ref[D_FOX + D_SB:D_FOX + D_SB + D_SSD, :]))
    h = _rms(x, g_ref[...]).astype(BF16)

    def cols(c):
        return slice(c * fc, (c + 1) * fc), slice(D_FF + c * fc, D_FF + (c + 1) * fc)

    def up(c):
        for half, cs in enumerate(cols(c)):
            hs = slice(half * fc, (half + 1) * fc)
            u_ref[c % 2, 0:8, hs] = carry_ref[:, cs]
            u_ref[c % 2, 8:8 + tm, hs] = _dot(h, wu_ref[:, cs])
            carry_ref[:, cs] = u_ref[c % 2, tm:tm + 8, hs]

    up(0)
    for c in range(nchunk):
        if c + 1 < nchunk:
            up(c + 1)
        ys = []
        for half, cs in enumerate(cols(c)):
            hs = slice(half * fc, (half + 1) * fc)
            y = cb_ref[:, cs] + cw_ref[FFN_CONV - 1:FFN_CONV, cs] * u_ref[c % 2, 8:8 + tm, hs]
            for k in range(1, FFN_CONV):
                y = y + cw_ref[FFN_CONV - 1 - k:FFN_CONV - k, cs] * u_ref[c % 2, 8 - k:8 - k + tm, hs]
            ys.append(y)
        gate, val = ys
        act_ref[:, c * fc:(c + 1) * fc] = (gate * _sigmoid(gate) * val).astype(BF16)
    out = x + _dot(act_ref[...], wd_ref[...])
    if final_norm:
        out = _rms(out, fg_ref[...])
    o_ref[...] = out


def _ffn(x2, yf, ys, yd, wo_all, g, wu_all, cw, cb, wd_all, fg, layer, seq, final_norm):
    t = x2.shape[0]
    tm = TM_PROJ
    tok = lambda n: pl.BlockSpec((tm, n), lambda i: (i, 0))
    row = lambda n: pl.BlockSpec((1, n), lambda i: (0, 0))
    return pl.pallas_call(
        functools.partial(_ffn_kernel, tm=tm, tiles_per_seq=seq // tm, final_norm=final_norm),
        grid=(t // tm,),
        in_specs=[
            tok(D_MODEL), tok(D_FOX), tok(D_SB), tok(D_SSD),
            _layer_weight(D_FOX + D_SB + D_SSD, D_MODEL, layer),
            row(D_MODEL),
            _layer_weight(D_MODEL, 2 * D_FF, layer),
            pl.BlockSpec((FFN_CONV, 2 * D_FF), lambda i: (0, 0)),
            row(2 * D_FF),
            _layer_weight(D_FF, D_MODEL, layer),
            row(D_MODEL),
        ],
        out_specs=tok(D_MODEL),
        out_shape=jax.ShapeDtypeStruct((t, D_MODEL), F32),
        scratch_shapes=[
            pltpu.VMEM((8, 2 * D_FF), F32),
            pltpu.VMEM((2, tm + 8, 2 * FFN_FC), F32),
            pltpu.VMEM((tm, D_FF), BF16),
        ],
        compiler_params=_cparams(("arbitrary",)),
        name="ffn",
    )(x2, yf, ys, yd, wo_all, g, wu_all, cw, cb, wd_all, fg)


def _pad_lanes(v, offset):
    return jnp.zeros((1, LANES), F32).at[0, offset:offset + v.shape[0]].set(v.astype(F32))


def _layer(x2, b, s, layer, wm_all, ws_all, wo_all, wu_all, wd_all, mix_g, fox_f_bias, fox_out_g, sb_out_g,
           ssd_conv_w, ssd_conv_b, ssd_dt_bias, ssd_a_log, ssd_d, ssd_norm_g, ffn_g, ffn_conv_w, ffn_conv_b,
           final_g, final_norm):
    main2, small2 = _inproj(x2, mix_g[None, :], wm_all, ws_all, layer, ssd_conv_w, ssd_conv_b[None, :], s)
    main3 = main2.reshape(b, s, N_MAIN)
    small3 = small2.reshape(b, s, LANES)

    c_full, cp = _fox_gate(small3, _pad_lanes(fox_f_bias, SMALL_FF))
    y_fox = _fox_attention(main3, cp, c_full, fox_out_g[None, :])
    y_sb = _sb_attention(main3, sb_out_g[None, :])

    e01 = (jnp.arange(2 * LANES)[:, None] % LANES == (jnp.arange(D_SSD)[None, :] // HEAD_DIM)).astype(BF16)
    y_ssd = _ssd(main3, small3, _pad_lanes(ssd_dt_bias, SMALL_DT), _pad_lanes(ssd_a_log, SMALL_DT),
                 jnp.repeat(ssd_d, HEAD_DIM)[None, :], ssd_norm_g[None, :], e01)

    t = b * s
    return _ffn(x2, y_fox.reshape(t, D_FOX), y_sb.reshape(t, D_SB), y_ssd.reshape(t, D_SSD), wo_all,
                ffn_g[None, :], wu_all, ffn_conv_w, ffn_conv_b[None, :], wd_all, final_g[None, :],
                layer, s, final_norm)


def kernel(x, mix_norm_g, w_in, fox_f_bias, fox_out_g, sb_out_g, ssd_conv_w, ssd_conv_b, ssd_dt_bias,
           ssd_a_log, ssd_d, ssd_norm_g, w_out, ffn_norm_g, w_up, ffn_conv_w, ffn_conv_b, w_down,
           final_norm_g):
    b, s, d = x.shape
    depth = w_in.shape[0]
    ff0 = 3 * D_FOX
    dt0 = N_MAIN + FOX_HEADS
    wm_all = jnp.concatenate([w_in[:, :, :ff0], w_in[:, :, ff0 + FOX_HEADS:dt0]], axis=2).astype(BF16)
    ws_all = jnp.concatenate([w_in[:, :, dt0:dt0 + SSD_HEADS], w_in[:, :, ff0:ff0 + FOX_HEADS],
                              jnp.zeros((depth, D_MODEL, LANES - SSD_HEADS - FOX_HEADS), F32)],
                             axis=2).astype(BF16)
    wo_all, wu_all, wd_all = w_out.astype(BF16), w_up.astype(BF16), w_down.astype(BF16)
    x2 = x.reshape(b * s, d)
    for l in range(depth):
        x2 = _layer(x2, b, s, l, wm_all, ws_all, wo_all, wu_all, wd_all, mix_norm_g[l], fox_f_bias[l],
                    fox_out_g[l], sb_out_g[l], ssd_conv_w[l], ssd_conv_b[l], ssd_dt_bias[l], ssd_a_log[l],
                    ssd_d[l], ssd_norm_g[l], ffn_norm_g[l], ffn_conv_w[l], ffn_conv_b[l], final_norm_g,
                    l == depth - 1)
    return x2.reshape(b, s, d)
```

```python
import functools

import jax
import jax.numpy as jnp
from jax import lax
from jax.experimental import pallas as pl
from jax.experimental.pallas import tpu as pltpu

F32 = jnp.float32
BF16 = jnp.bfloat16

D_MODEL = 1024
HEAD_DIM = 64
FOX_HEADS = 8
SB_HEADS = 8
SSD_HEADS = 16
SSD_GROUPS = 2
SSD_STATE = 128
SSD_CONV = 4
SSD_CHUNK = 128
D_FOX = FOX_HEADS * HEAD_DIM
D_SB = SB_HEADS * HEAD_DIM
D_SSD = SSD_HEADS * HEAD_DIM
D_BC = SSD_GROUPS * SSD_STATE
D_FF = 2816
FFN_CONV = 3
NORM_EPS = 1e-6
CP_TERMS = 3
EXP_UNDERFLOW = -104.0

LANES = 128
N_MAIN = 3 * D_FOX + 3 * D_SB + D_SSD + D_SSD + 2 * D_BC
OFF_FOX = 0
OFF_SB = 3 * D_FOX
OFF_Z = OFF_SB + 3 * D_SB
OFF_XS = OFF_Z + D_SSD
OFF_BC = OFF_XS + D_SSD
SMALL_DT = 0
SMALL_FF = SSD_HEADS

VMEM_LIMIT = 56 * 1024 * 1024

ATTN_SCALE = HEAD_DIM ** -0.5
TM_PROJ = 512
PROJ_CHUNK = 512
GATE_ROWS = 512
TQ = 256
TQ_FOX = 512
SSD_CHUNKS_PER_STEP = 8
EXPAND_PAD = 16
ROW_BLOCK = 32
ATTN_PAIRS_PER_STEP = 2
FOX_PAIRS_PER_STEP = 2
SB_PAIRS_PER_STEP = 4
FFN_FC = 256


def _cparams(sem):
    return pltpu.CompilerParams(dimension_semantics=sem, vmem_limit_bytes=VMEM_LIMIT)


def _layer_weight(rows, cols, layer):
    return pl.BlockSpec((None, rows, cols), lambda i: (layer, 0, 0), pipeline_mode=pl.Buffered(1))


def _split_bf16(x, n):
    parts, r = [], x
    for _ in range(n):
        p = r.astype(BF16)
        parts.append(p)
        r = r - p.astype(F32)
    return parts


def _dot(a, b):
    return jnp.dot(a, b, preferred_element_type=F32)


def _dot_nt(a, b):
    return lax.dot_general(a, b, (((1,), (1,)), ((), ())), preferred_element_type=F32)


def _dot_tn(a, b):
    return lax.dot_general(a, b, (((0,), (0,)), ((), ())), preferred_element_type=F32)


def _mask_dot_left(m01, x, n):
    out = None
    for p in _split_bf16(x, n):
        t = _dot(m01, p)
        out = t if out is None else out + t
    return out


def _softplus_neg_abs(x):
    return jnp.log(1.0 + jnp.exp(-jnp.abs(x)))


def _sigmoid(x):
    return 1.0 / (1.0 + jnp.exp(-x))


def _rms(x, g):
    ms = jnp.mean(x * x, axis=-1, keepdims=True)
    return x * lax.rsqrt(ms + NORM_EPS) * g


def _head_pair_norm(o2, g, lane):
    lo = lane < HEAD_DIM
    sq = o2 * o2
    ms0 = jnp.sum(jnp.where(lo, sq, 0.0), axis=-1, keepdims=True) * (1.0 / HEAD_DIM)
    ms1 = jnp.sum(jnp.where(lo, 0.0, sq), axis=-1, keepdims=True) * (1.0 / HEAD_DIM)
    ms = jnp.where(lo, ms0, ms1)
    return o2 * lax.rsqrt(ms + NORM_EPS) * g


def _inproj_kernel(x_ref, g_ref, wm_ref, ws_ref, cw_ref, cb_ref, main_ref, small_ref, carry_ref, u_ref,
                   *, tm, nc, tiles_per_seq):
    @pl.when(pl.program_id(0) % tiles_per_seq == 0)
    def _():
        carry_ref[...] = jnp.zeros_like(carry_ref)

    h = _rms(x_ref[...], g_ref[...]).astype(BF16)
    order = list(reversed(range(0, N_MAIN, nc)))

    def project(n):
        c = order[n]
        u_ref[n % 2, 8:8 + tm, :] = _dot(h, wm_ref[:, c:c + nc])
        if c >= OFF_XS:
            cc = slice(c - OFF_XS, c - OFF_XS + nc)
            u_ref[n % 2, 0:8, :] = carry_ref[:, cc]
            carry_ref[:, cc] = u_ref[n % 2, tm:tm + 8, :]

    project(0)
    for n, c in enumerate(order):
        if n + 1 < len(order):
            project(n + 1)
        acc = u_ref[n % 2, 8:8 + tm, :]
        if c >= OFF_XS:
            cc = slice(c - OFF_XS, c - OFF_XS + nc)
            acc = cb_ref[:, cc] + cw_ref[SSD_CONV - 1:SSD_CONV, cc] * acc
            for k in range(1, SSD_CONV):
                acc = acc + cw_ref[SSD_CONV - 1 - k:SSD_CONV - k, cc] * u_ref[n % 2, 8 - k:8 - k + tm, :]
        if c >= OFF_Z:
            acc = acc * _sigmoid(acc)
        main_ref[:, c:c + nc] = acc.astype(BF16)
    small_ref[...] = _dot(h, ws_ref[...])


def _inproj(x2, g, wm_all, ws_all, layer, cw, cb, seq):
    t = x2.shape[0]
    tm = TM_PROJ
    nc = PROJ_CHUNK
    n_conv = N_MAIN - OFF_XS
    return pl.pallas_call(
        functools.partial(_inproj_kernel, tm=tm, nc=nc, tiles_per_seq=seq // tm),
        grid=(t // tm,),
        in_specs=[
            pl.BlockSpec((tm, D_MODEL), lambda i: (i, 0)),
            pl.BlockSpec((1, D_MODEL), lambda i: (0, 0)),
            _layer_weight(D_MODEL, N_MAIN, layer),
            _layer_weight(D_MODEL, LANES, layer),
            pl.BlockSpec((SSD_CONV, n_conv), lambda i: (0, 0)),
            pl.BlockSpec((1, n_conv), lambda i: (0, 0)),
        ],
        out_specs=[
            pl.BlockSpec((tm, N_MAIN), lambda i: (i, 0)),
            pl.BlockSpec((tm, LANES), lambda i: (i, 0)),
        ],
        out_shape=[
            jax.ShapeDtypeStruct((t, N_MAIN), BF16),
            jax.ShapeDtypeStruct((t, LANES), F32),
        ],
        scratch_shapes=[
            pltpu.VMEM((8, n_conv), F32),
            pltpu.VMEM((2, tm + 8, nc), F32),
        ],
        compiler_params=_cparams(("arbitrary",)),
        name="inproj",
    )(x2, g, wm_all, ws_all, cw, cb)


def _fox_gate_kernel(s_ref, bias_ref, c_ref, cp_ref, carry_ref, *, tb):
    @pl.when(pl.program_id(1) == 0)
    def _():
        carry_ref[...] = jnp.zeros_like(carry_ref)

    xx = s_ref[0] + bias_ref[...]
    log_f = jnp.minimum(xx, 0.0) - _softplus_neg_abs(xx)
    row = lax.broadcasted_iota(jnp.int32, (tb, tb), 0)
    col = lax.broadcasted_iota(jnp.int32, (tb, tb), 1)
    tri = jnp.where(row >= col, 1.0, 0.0).astype(BF16)
    cum = _mask_dot_left(tri, log_f, 3) + carry_ref[...]
    c_ref[0] = cum
    carry_ref[...] = cum[tb - 1:tb, :]
    src = lax.broadcasted_iota(jnp.int32, (LANES, LANES), 0) - SMALL_FF
    dst = lax.broadcasted_iota(jnp.int32, (LANES, LANES), 1)
    head_ok = (src >= 0) & (src < FOX_HEADS)
    cp = None
    for j, part in enumerate(_split_bf16(-cum, CP_TERMS)):
        sel = jnp.where(head_ok & (dst == CP_TERMS * src + j), 1.0, 0.0).astype(BF16)
        t = _dot(part, sel)
        cp = t if cp is None else cp + t
    cp_ref[0] = cp.astype(BF16)


def _fox_gate(small3, bias_row):
    b, s, _ = small3.shape
    tb = GATE_ROWS
    blk = pl.BlockSpec((1, tb, LANES), lambda bi, i: (bi, i, 0))
    return pl.pallas_call(
        functools.partial(_fox_gate_kernel, tb=tb),
        grid=(b, s // tb),
        in_specs=[blk, pl.BlockSpec((1, LANES), lambda bi, i: (0, 0))],
        out_specs=[blk, blk],
        out_shape=[jax.ShapeDtypeStruct((b, s, LANES), F32),
                   jax.ShapeDtypeStruct((b, s, LANES), BF16)],
        scratch_shapes=[pltpu.VMEM((1, LANES), F32)],
        compiler_params=_cparams(("arbitrary", "arbitrary")),
        name="fox_gate",
    )(small3, bias_row)


def _attn_kernel(fq_ref, fk_ref, fv_ref, cp_ref, cq_ref, fg_ref, sq_ref, sk_ref, sv_ref, sg_ref,
                 fo_ref, so_ref,
                 vh_ref, m_ref, facc_ref, c_ref, alpha_ref, t_ref, p_ref,
                 r_ref, sacc_ref, z_ref, lb_ref, lk_ref, w_ref, *, tq, ts, npairs):
    p_idx = pl.program_id(1)
    i = pl.program_id(2)
    lane = lax.broadcasted_iota(jnp.int32, (1, LANES), 1)
    lo_half = lane < HEAD_DIM
    nheads = 2 * npairs
    nblk = tq // ts
    pair = lambda h: slice((h // 2) * LANES, (h // 2 + 1) * LANES)

    @pl.when(i == 0)
    def _():
        for pr in range(npairs):
            v2 = fv_ref[0, :, pair(2 * pr)]
            one = jnp.ones_like(v2)
            vh_ref[2 * pr] = jnp.where(lo_half, v2, one)
            vh_ref[2 * pr + 1] = jnp.where(lo_half, one, v2)

    q_aug = []
    for h in range(nheads):
        q2 = fq_ref[0, :, pair(h)]
        own = lo_half if h % 2 == 0 else jnp.logical_not(lo_half)
        first = CP_TERMS * (nheads * p_idx + h)
        ones_at = jnp.where((lane >= first) & (lane < first + CP_TERMS), 1.0, 0.0).astype(BF16)
        q_aug.append(jnp.concatenate(
            [jnp.where(own, q2, jnp.zeros_like(q2)) * ATTN_SCALE, jnp.broadcast_to(ones_at, (tq, LANES))],
            axis=1))
        c_lane = SMALL_FF + nheads * p_idx + h
        c_ref[h] = jnp.broadcast_to(
            jnp.sum(jnp.where(lane == c_lane, cq_ref[0], 0.0), axis=-1, keepdims=True), (tq, LANES))

    m_ref[...] = jnp.full_like(m_ref, -jnp.inf)
    facc_ref[...] = jnp.zeros_like(facc_ref)

    def fox_tile(start, tk, masked):
        cp_t = cp_ref[0, pl.ds(start, tk), :]
        for h in range(nheads):
            k_aug = jnp.concatenate([fk_ref[0, pl.ds(start, tk), pair(h)], cp_t], axis=1)
            t_ref[h, :, 0:tk] = _dot_nt(q_aug[h], k_aug)
        for h in range(nheads):
            for r in range(0, tq, ROW_BLOCK):
                rows = slice(r, r + ROW_BLOCK)
                t = t_ref[h, rows, 0:tk]
                if masked:
                    row = lax.broadcasted_iota(jnp.int32, (ROW_BLOCK, tk), 0) + r
                    col = lax.broadcasted_iota(jnp.int32, (ROW_BLOCK, tk), 1)
                    t = jnp.where(row >= col, t, -jnp.inf)
                m_prev = m_ref[h, rows, :]
                m_new = jnp.maximum(m_prev, jnp.max(t, axis=-1, keepdims=True) + c_ref[h, rows, :])
                shift = m_new - c_ref[h, rows, :]
                for c in range(0, tk, LANES):
                    p_ref[h, rows, c:c + LANES] = jnp.exp((t[:, c:c + LANES] - shift).astype(BF16))
                alpha_ref[h, rows, :] = jnp.exp(m_prev - m_new)
                m_ref[h, rows, :] = m_new
        for h in range(nheads):
            facc_ref[h] = alpha_ref[h] * facc_ref[h] + _dot(p_ref[h, :, 0:tk], vh_ref[h, pl.ds(start, tk), :])

    qh = []
    for blk in range(nblk):
        for pr in range(npairs):
            q2 = sq_ref[0, blk * ts:(blk + 1) * ts, pair(2 * pr)]
            zero = jnp.zeros_like(q2)
            qh += [jnp.where(lo_half, q2, zero) * ATTN_SCALE, jnp.where(lo_half, zero, q2) * ATTN_SCALE]
    row_s = lax.broadcasted_iota(jnp.int32, (ts, ts), 0)
    col_s = lax.broadcasted_iota(jnp.int32, (ts, ts), 1)
    later = jnp.where(row_s > col_s, 1.0, 0.0).astype(BF16)

    r_ref[...] = jnp.zeros_like(r_ref)
    sacc_ref[...] = jnp.zeros_like(sacc_ref)

    def sb_tiles(jobs):
        units = [(blk * nheads + h, h, pl.multiple_of(j * ts, ts), masked)
                 for blk, j, masked in jobs for h in range(nheads)]
        for s, h, start, _ in units:
            z_ref[s] = _dot_nt(qh[s], sk_ref[0, pl.ds(start, ts), pair(h)])
        r_prev = {}
        for s, h, start, masked in units:
            z = z_ref[s]
            log_beta = jnp.minimum(z, 0.0) - _softplus_neg_abs(z)
            log_keep = log_beta - z
            if masked:
                log_keep = jnp.where(col_s < row_s, log_keep, 0.0)
            lb_ref[s] = log_beta
            lk_ref[s] = log_keep.astype(BF16)
            r_prev[s] = r_ref[s]
            r_ref[s] = r_prev[s] + jnp.sum(log_keep, axis=-1, keepdims=True)
        for s, h, start, _ in units:
            z_ref[s] = _dot(lk_ref[s], later)
        for s, h, start, masked in units:
            log_w = lb_ref[s] + z_ref[s]
            chunks = []
            for c in range(0, ts, LANES):
                lw = log_w[:, c:c + LANES] + r_prev[s]
                if masked:
                    row_c = lax.broadcasted_iota(jnp.int32, (ts, LANES), 0)
                    col_c = lax.broadcasted_iota(jnp.int32, (ts, LANES), 1) + c
                    lw = jnp.where(col_c < row_c, lw, -jnp.inf)
                chunks.append(jnp.exp(lw).astype(BF16))
            w_ref[s] = jnp.concatenate(chunks, axis=1)
        for s, h, start, _ in units:
            sacc_ref[s] = sacc_ref[s] + _dot(w_ref[s], sv_ref[0, pl.ds(start, ts), pair(h)])

    def fox_body(j, carry):
        fox_tile(pl.multiple_of(j * (2 * tq), 2 * tq), 2 * tq, False)
        return carry

    lax.fori_loop(0, i // 2, fox_body, 0)

    @pl.when(i % 2 == 1)
    def _():
        fox_tile(pl.multiple_of((i - 1) * tq, tq), tq, False)

    fox_tile(pl.multiple_of(i * tq, tq), tq, True)
    sb_tiles([(blk, i * nblk + blk, True) for blk in range(nblk)])
    sb_tiles([(blk, i * nblk + blk - 1, False) for blk in range(1, nblk)])

    for pr in range(npairs):
        a0, a1 = facc_ref[2 * pr], facc_ref[2 * pr + 1]
        num = jnp.where(lo_half, a0, a1)
        den = pltpu.roll(jnp.where(lo_half, a1, a0), HEAD_DIM, axis=1)
        fo_ref[0, :, pair(2 * pr)] = _head_pair_norm(num / den, fg_ref[:, pair(2 * pr)], lane).astype(BF16)

    for blk in range(nblk):
        done = 1 if blk > 0 else 0
        first = i * nblk + blk - 1 - done
        states = slice(blk * nheads, (blk + 1) * nheads)

        def cond(carry, first=first):
            it, live = carry
            return jnp.logical_and(it <= first, live)

        def body(carry, blk=blk, first=first, states=states):
            it, _ = carry
            sb_tiles([(blk, first - it, False)])
            return it + 1, jnp.max(r_ref[states]) > EXP_UNDERFLOW

        lax.while_loop(cond, body, (0, jnp.max(r_ref[states]) > EXP_UNDERFLOW))

    for blk in range(nblk):
        for pr in range(npairs):
            s0 = blk * nheads + 2 * pr
            o2 = jnp.where(lo_half, sacc_ref[s0], sacc_ref[s0 + 1])
            so_ref[0, blk * ts:(blk + 1) * ts, pair(2 * pr)] = _head_pair_norm(
                o2, sg_ref[:, pair(2 * pr)], lane).astype(BF16)


def _attention(main3, cp, cq, fox_gain, sb_gain):
    b, s, _ = main3.shape
    tq, ts = TQ_FOX, TQ
    npairs = ATTN_PAIRS_PER_STEP
    nheads = 2 * npairs
    nstate = nheads * (tq // ts)
    w = npairs * LANES
    nsteps = D_FOX // w
    fqb, fkb, fvb = OFF_FOX // w, (OFF_FOX + D_FOX) // w, (OFF_FOX + 2 * D_FOX) // w
    sqb, skb, svb = OFF_SB // w, (OFF_SB + D_SB) // w, (OFF_SB + 2 * D_SB) // w
    qblk = lambda c0: pl.BlockSpec((1, tq, w), lambda bi, p, i: (bi, i, c0 + p))
    seq = lambda c0: pl.BlockSpec((1, s, w), lambda bi, p, i: (bi, 0, c0 + p))
    seq1 = lambda c0: pl.BlockSpec((1, s, w), lambda bi, p, i: (bi, 0, c0 + p), pipeline_mode=pl.Buffered(1))
    gain = pl.BlockSpec((1, w), lambda bi, p, i: (0, p))
    out = pl.BlockSpec((1, tq, w), lambda bi, p, i: (bi, i, p))
    return pl.pallas_call(
        functools.partial(_attn_kernel, tq=tq, ts=ts, npairs=npairs),
        grid=(b, nsteps, s // tq),
        in_specs=[
            qblk(fqb), seq(fkb), seq1(fvb),
            pl.BlockSpec((1, s, LANES), lambda bi, p, i: (bi, 0, 0), pipeline_mode=pl.Buffered(1)),
            pl.BlockSpec((1, tq, LANES), lambda bi, p, i: (bi, i, 0)),
            gain,
            qblk(sqb), seq(skb), seq(svb), gain,
        ],
        out_specs=[out, out],
        out_shape=[jax.ShapeDtypeStruct((b, s, D_FOX), BF16), jax.ShapeDtypeStruct((b, s, D_SB), BF16)],
        scratch_shapes=[
            pltpu.VMEM((nheads, s, LANES), BF16),
            pltpu.VMEM((nheads, tq, LANES), F32),
            pltpu.VMEM((nheads, tq, LANES), F32),
            pltpu.VMEM((nheads, tq, LANES), F32),
            pltpu.VMEM((nheads, tq, LANES), F32),
            pltpu.VMEM((nheads, tq, 2 * tq), F32),
            pltpu.VMEM((nheads, tq, 2 * tq), BF16),
            pltpu.VMEM((nstate, ts, LANES), F32),
            pltpu.VMEM((nstate, ts, LANES), F32),
            pltpu.VMEM((nstate, ts, ts), F32),
            pltpu.VMEM((nstate, ts, ts), F32),
            pltpu.VMEM((nstate, ts, ts), BF16),
            pltpu.VMEM((nstate, ts, ts), BF16),
        ],
        compiler_params=_cparams(("arbitrary", "arbitrary", "arbitrary")),
        name="attn",
    )(main3, main3, main3, cp, cq, fox_gain, main3, main3, main3, sb_gain)


def _fox_kernel(q_ref, k_ref, v_ref, cp_ref, cq_ref, g_ref, o_ref, vh_ref, m_ref, acc_ref, c_ref,
                alpha_ref, t_ref, p_ref, *, tq, npairs):
    p_idx = pl.program_id(1)
    i = pl.program_id(2)
    lane = lax.broadcasted_iota(jnp.int32, (1, LANES), 1)
    lo_half = lane < HEAD_DIM
    nheads = 2 * npairs
    pair = lambda h: slice((h // 2) * LANES, (h // 2 + 1) * LANES)

    @pl.when(i == 0)
    def _():
        for pr in range(npairs):
            v2 = v_ref[0, :, pair(2 * pr)]
            one = jnp.ones_like(v2)
            vh_ref[2 * pr] = jnp.where(lo_half, v2, one)
            vh_ref[2 * pr + 1] = jnp.where(lo_half, one, v2)

    q_aug = []
    for h in range(nheads):
        q2 = q_ref[0, :, pair(h)]
        own = lo_half if h % 2 == 0 else jnp.logical_not(lo_half)
        first = CP_TERMS * (nheads * p_idx + h)
        ones_at = jnp.where((lane >= first) & (lane < first + CP_TERMS), 1.0, 0.0).astype(BF16)
        q_aug.append(jnp.concatenate(
            [jnp.where(own, q2, jnp.zeros_like(q2)) * ATTN_SCALE, jnp.broadcast_to(ones_at, (tq, LANES))],
            axis=1))
        c_lane = SMALL_FF + nheads * p_idx + h
        c_ref[h] = jnp.broadcast_to(
            jnp.sum(jnp.where(lane == c_lane, cq_ref[0], 0.0), axis=-1, keepdims=True), (tq, LANES))

    m_ref[...] = jnp.full_like(m_ref, -jnp.inf)
    acc_ref[...] = jnp.zeros_like(acc_ref)

    def tile(start, tk, masked):
        cp_t = cp_ref[0, pl.ds(start, tk), :]
        for h in range(nheads):
            k_aug = jnp.concatenate([k_ref[0, pl.ds(start, tk), pair(h)], cp_t], axis=1)
            t_ref[h, :, 0:tk] = _dot_nt(q_aug[h], k_aug)
        for h in range(nheads):
            for r in range(0, tq, ROW_BLOCK):
                rows = slice(r, r + ROW_BLOCK)
                t = t_ref[h, rows, 0:tk]
                if masked:
                    row = lax.broadcasted_iota(jnp.int32, (ROW_BLOCK, tk), 0) + r
                    col = lax.broadcasted_iota(jnp.int32, (ROW_BLOCK, tk), 1)
                    t = jnp.where(row >= col, t, -jnp.inf)
                m_prev = m_ref[h, rows, :]
                m_new = jnp.maximum(m_prev, jnp.max(t, axis=-1, keepdims=True) + c_ref[h, rows, :])
                shift = m_new - c_ref[h, rows, :]
                for c in range(0, tk, LANES):
                    p_ref[h, rows, c:c + LANES] = jnp.exp((t[:, c:c + LANES] - shift).astype(BF16))
                alpha_ref[h, rows, :] = jnp.exp(m_prev - m_new)
                m_ref[h, rows, :] = m_new
        for h in range(nheads):
            acc_ref[h] = alpha_ref[h] * acc_ref[h] + _dot(p_ref[h, :, 0:tk], vh_ref[h, pl.ds(start, tk), :])

    def body(j, carry):
        tile(pl.multiple_of(j * (2 * tq), 2 * tq), 2 * tq, False)
        return carry

    lax.fori_loop(0, i // 2, body, 0)

    @pl.when(i % 2 == 1)
    def _():
        tile(pl.multiple_of((i - 1) * tq, tq), tq, False)

    tile(pl.multiple_of(i * tq, tq), tq, True)

    for pr in range(npairs):
        a0, a1 = acc_ref[2 * pr], acc_ref[2 * pr + 1]
        num = jnp.where(lo_half, a0, a1)
        den = pltpu.roll(jnp.where(lo_half, a1, a0), HEAD_DIM, axis=1)
        o_ref[0, :, pair(2 * pr)] = _head_pair_norm(num / den, g_ref[:, pair(2 * pr)], lane).astype(BF16)


def _fox_attention(main3, cp, cq, gain):
    b, s, _ = main3.shape
    tq = TQ_FOX
    npairs = FOX_PAIRS_PER_STEP
    nheads = 2 * npairs
    w = npairs * LANES
    nsteps = D_FOX // w
    qb, kb, vb = OFF_FOX // w, (OFF_FOX + D_FOX) // w, (OFF_FOX + 2 * D_FOX) // w
    return pl.pallas_call(
        functools.partial(_fox_kernel, tq=tq, npairs=npairs),
        grid=(b, nsteps, s // tq),
        in_specs=[
            pl.BlockSpec((1, tq, w), lambda bi, p, i: (bi, i, qb + p)),
            pl.BlockSpec((1, s, w), lambda bi, p, i: (bi, 0, kb + p)),
            pl.BlockSpec((1, s, w), lambda bi, p, i: (bi, 0, vb + p)),
            pl.BlockSpec((1, s, LANES), lambda bi, p, i: (bi, 0, 0)),
            pl.BlockSpec((1, tq, LANES), lambda bi, p, i: (bi, i, 0)),
            pl.BlockSpec((1, w), lambda bi, p, i: (0, p)),
        ],
        out_specs=pl.BlockSpec((1, tq, w), lambda bi, p, i: (bi, i, p)),
        out_shape=jax.ShapeDtypeStruct((b, s, D_FOX), BF16),
        scratch_shapes=[
            pltpu.VMEM((nheads, s, LANES), BF16),
            pltpu.VMEM((nheads, tq, LANES), F32),
            pltpu.VMEM((nheads, tq, LANES), F32),
            pltpu.VMEM((nheads, tq, LANES), F32),
            pltpu.VMEM((nheads, tq, LANES), F32),
            pltpu.VMEM((nheads, tq, 2 * tq), F32),
            pltpu.VMEM((nheads, tq, 2 * tq), BF16),
        ],
        compiler_params=_cparams(("arbitrary", "arbitrary", "arbitrary")),
        name="fox_attn",
    )(main3, main3, main3, cp, cq, gain)


def _sb_kernel(q_ref, k_ref, v_ref, g_ref, o_ref, r_ref, acc_ref, z_ref, lb_ref, lk_ref, w_ref, *,
               tq, npairs):
    i = pl.program_id(2)
    lane = lax.broadcasted_iota(jnp.int32, (1, LANES), 1)
    nheads = 2 * npairs
    qh = []
    for pr in range(npairs):
        q2 = q_ref[0, :, pr * LANES:(pr + 1) * LANES]
        zero = jnp.zeros_like(q2)
        qh += [jnp.where(lane < HEAD_DIM, q2, zero) * ATTN_SCALE,
               jnp.where(lane < HEAD_DIM, zero, q2) * ATTN_SCALE]
    row = lax.broadcasted_iota(jnp.int32, (tq, tq), 0)
    col = lax.broadcasted_iota(jnp.int32, (tq, tq), 1)
    later = jnp.where(row > col, 1.0, 0.0).astype(BF16)

    r_ref[...] = jnp.zeros_like(r_ref)
    acc_ref[...] = jnp.zeros_like(acc_ref)

    def tile(j, masked):
        start = pl.multiple_of(j * tq, tq)
        pair = lambda hh: slice((hh // 2) * LANES, (hh // 2 + 1) * LANES)
        for hh in range(nheads):
            z_ref[hh] = _dot_nt(qh[hh], k_ref[0, pl.ds(start, tq), pair(hh)])
        r_prev = []
        for hh in range(nheads):
            z = z_ref[hh]
            log_beta = jnp.minimum(z, 0.0) - _softplus_neg_abs(z)
            log_keep = log_beta - z
            if masked:
                log_keep = jnp.where(col < row, log_keep, 0.0)
            lb_ref[hh] = log_beta
            lk_ref[hh] = log_keep.astype(BF16)
            r_prev.append(r_ref[hh])
            r_ref[hh] = r_prev[hh] + jnp.sum(log_keep, axis=-1, keepdims=True)
        for hh in range(nheads):
            z_ref[hh] = _dot(lk_ref[hh], later)
        for hh in range(nheads):
            log_w = lb_ref[hh] + z_ref[hh]
            chunks = []
            for c in range(0, tq, LANES):
                lw = log_w[:, c:c + LANES] + r_prev[hh]
                if masked:
                    row_c = lax.broadcasted_iota(jnp.int32, (tq, LANES), 0)
                    col_c = lax.broadcasted_iota(jnp.int32, (tq, LANES), 1) + c
                    lw = jnp.where(col_c < row_c, lw, -jnp.inf)
                chunks.append(jnp.exp(lw).astype(BF16))
            w_ref[hh] = jnp.concatenate(chunks, axis=1)
        for hh in range(nheads):
            acc_ref[hh] = acc_ref[hh] + _dot(w_ref[hh], v_ref[0, pl.ds(start, tq), pair(hh)])

    tile(i, True)

    def cond(carry):
        it, live = carry
        return jnp.logical_and(it < i, live)

    def body(carry):
        it, _ = carry
        tile(i - 1 - it, False)
        return it + 1, jnp.max(r_ref[...]) > EXP_UNDERFLOW

    lax.while_loop(cond, body, (0, True))

    for pr in range(npairs):
        ps = slice(pr * LANES, (pr + 1) * LANES)
        o2 = jnp.where(lane < HEAD_DIM, acc_ref[2 * pr], acc_ref[2 * pr + 1])
        o_ref[0, :, ps] = _head_pair_norm(o2, g_ref[:, ps], lane).astype(BF16)


def _sb_attention(main3, gain):
    b, s, _ = main3.shape
    tq = TQ
    npairs = SB_PAIRS_PER_STEP
    nheads = 2 * npairs
    w = npairs * LANES
    nsteps = D_SB // w
    qb, kb, vb = OFF_SB // w, (OFF_SB + D_SB) // w, (OFF_SB + 2 * D_SB) // w
    return pl.pallas_call(
        functools.partial(_sb_kernel, tq=tq, npairs=npairs),
        grid=(b, nsteps, s // tq),
        in_specs=[
            pl.BlockSpec((1, tq, w), lambda bi, p, i: (bi, i, qb + p)),
            pl.BlockSpec((1, s, w), lambda bi, p, i: (bi, 0, kb + p)),
            pl.BlockSpec((1, s, w), lambda bi, p, i: (bi, 0, vb + p)),
            pl.BlockSpec((1, w), lambda bi, p, i: (0, p)),
        ],
        out_specs=pl.BlockSpec((1, tq, w), lambda bi, p, i: (bi, i, p)),
        out_shape=jax.ShapeDtypeStruct((b, s, D_SB), BF16),
        scratch_shapes=[
            pltpu.VMEM((nheads, tq, LANES), F32),
            pltpu.VMEM((nheads, tq, LANES), F32),
            pltpu.VMEM((nheads, tq, tq), F32),
            pltpu.VMEM((nheads, tq, tq), F32),
            pltpu.VMEM((nheads, tq, tq), BF16),
            pltpu.VMEM((nheads, tq, tq), BF16),
        ],
        compiler_params=_cparams(("arbitrary", "arbitrary", "arbitrary")),
        name="sb_attn",
    )(main3, main3, main3, gain)


def _ssd_kernel(z_ref, xs_ref, bc_ref, dt_ref, dtb_ref, alog_ref, dexp_ref, ng_ref, e_ref, o_ref, state_ref):
    L = SSD_CHUNK
    hpg = SSD_HEADS // SSD_GROUPS
    gw = hpg * HEAD_DIM

    @pl.when(pl.program_id(1) == 0)
    def _():
        state_ref[...] = jnp.zeros_like(state_ref)

    def chunk(rows):
        xdt_raw = dt_ref[0, rows, :] + dtb_ref[...]
        dt = jnp.maximum(xdt_raw, 0.0) + _softplus_neg_abs(xdt_raw)
        a = -jnp.exp(alog_ref[...])
        da = dt * a
        row = lax.broadcasted_iota(jnp.int32, (L, L), 0)
        col = lax.broadcasted_iota(jnp.int32, (L, L), 1)
        causal = row >= col
        tri = jnp.where(causal, 1.0, 0.0).astype(BF16)
        a_cs = _mask_dot_left(tri, da, 3)
        a_cs_t = a_cs.T
        a_last = a_cs[L - 1:L, :]

        per_head = jnp.concatenate([dt, jnp.exp(a_cs), jnp.exp(a_last - a_cs),
                                    jnp.broadcast_to(jnp.exp(a_last), (EXPAND_PAD, LANES))], axis=0)
        per_head = jnp.concatenate(_split_bf16(per_head, 2), axis=1)

        lane = lax.broadcasted_iota(jnp.int32, (1, LANES), 1)
        for g in range(SSD_GROUPS):
            sl = slice(g * gw, (g + 1) * gw)
            wide = _dot(per_head, e_ref[:, sl])
            dt_x = wide[0:L]
            dec_in_x = wide[L:2 * L]
            dec_end_x = wide[2 * L:3 * L]
            chunk_dec_x = wide[3 * L:3 * L + 1]

            xs = xs_ref[0, rows, sl].astype(F32)
            xdt = xs * dt_x
            xdt_b = xdt.astype(BF16)
            xend_b = (xdt * dec_end_x).astype(BF16)

            b_g = bc_ref[0, rows, g * SSD_STATE:(g + 1) * SSD_STATE]
            c_g = bc_ref[0, rows, D_BC + g * SSD_STATE:D_BC + (g + 1) * SSD_STATE]
            cb = _dot_nt(c_g, b_g)
            st = state_ref[g]
            y_off = _dot(c_g, st.astype(BF16)) * dec_in_x
            y_diag = []
            for pr in range(hpg // 2):
                outs = []
                for hh in range(2):
                    h = g * hpg + 2 * pr + hh
                    seg = jnp.exp(jnp.where(causal, a_cs[:, h:h + 1] - a_cs_t[h:h + 1, :], -jnp.inf))
                    m = (cb * seg).astype(BF16)
                    outs.append(_dot(m, xdt_b[:, pr * LANES:(pr + 1) * LANES]))
                y_diag.append(jnp.where(lane < HEAD_DIM, outs[0], outs[1]))
            state_ref[g] = chunk_dec_x * st + _dot_tn(b_g, xend_b)

            y_g = jnp.concatenate(y_diag, axis=-1) + y_off
            y_g = (y_g + xs * dexp_ref[:, sl]) * z_ref[0, rows, sl].astype(F32)
            o_ref[0, rows, sl] = _rms(y_g, ng_ref[:, sl]).astype(BF16)

    for sub in range(SSD_CHUNKS_PER_STEP):
        chunk(slice(sub * L, (sub + 1) * L))


def _ssd(main3, small3, dtb, alog, dexp, ng, e01):
    b, s, _ = main3.shape
    L = SSD_CHUNK * SSD_CHUNKS_PER_STEP
    zb, xb, bcb = OFF_Z // D_SSD, OFF_XS // D_SSD, OFF_BC // (2 * D_BC)
    const = lambda shape: pl.BlockSpec(shape, lambda bi, c: (0,) * len(shape))
    return pl.pallas_call(
        _ssd_kernel,
        grid=(b, s // L),
        in_specs=[
            pl.BlockSpec((1, L, D_SSD), lambda bi, c: (bi, c, zb)),
            pl.BlockSpec((1, L, D_SSD), lambda bi, c: (bi, c, xb)),
            pl.BlockSpec((1, L, 2 * D_BC), lambda bi, c: (bi, c, bcb)),
            pl.BlockSpec((1, L, LANES), lambda bi, c: (bi, c, 0)),
            const((1, LANES)), const((1, LANES)),
            const((1, D_SSD)), const((1, D_SSD)),
            const((2 * LANES, D_SSD)),
        ],
        out_specs=pl.BlockSpec((1, L, D_SSD), lambda bi, c: (bi, c, 0)),
        out_shape=jax.ShapeDtypeStruct((b, s, D_SSD), BF16),
        scratch_shapes=[pltpu.VMEM((SSD_GROUPS, SSD_STATE, D_SSD // SSD_GROUPS), F32)],
        compiler_params=_cparams(("arbitrary", "arbitrary")),
        name="ssd",
    )(main3, main3, main3, small3, dtb, alog, dexp, ng, e01)


def _ffn_kernel(x_ref, yf_ref, ys_ref, yd_ref, wo_ref, g_ref, wu_ref, cw_ref, cb_ref, wd_ref, fg_ref, o_ref,
                carry_ref, u_ref, act_ref, *, tm, tiles_per_seq, final_norm):
    fc = FFN_FC
    nchunk = D_FF // fc

    @pl.when(pl.program_id(0) % tiles_per_seq == 0)
    def _():
        carry_ref[...] = jnp.zeros_like(carry_ref)

    x = (x_ref[...] + _dot(yf_ref[...], wo_ref[0:D_FOX, :])
         + _dot(ys_ref[...], wo_ref[D_FOX:D_FOX + D_SB, :])
         + _dot(yd_ref[...], wo_ref[D_FOX + D_SB:D_FOX + D_SB + D_SSD, :]))
    h = _rms(x, g_ref[...]).astype(BF16)

    def cols(c):
        return slice(c * fc, (c + 1) * fc), slice(D_FF + c * fc, D_FF + (c + 1) * fc)

    def up(c):
        for half, cs in enumerate(cols(c)):
            hs = slice(half * fc, (half + 1) * fc)
            u_ref[c % 2, 0:8, hs] = carry_ref[:, cs]
            u_ref[c % 2, 8:8 + tm, hs] = _dot(h, wu_ref[:, cs])
            carry_ref[:, cs] = u_ref[c % 2, tm:tm + 8, hs]

    up(0)
    for c in range(nchunk):
        if c + 1 < nchunk:
            up(c + 1)
        ys = []
        for half, cs in enumerate(cols(c)):
            hs = slice(half * fc, (half + 1) * fc)
            y = cb_ref[:, cs] + cw_ref[FFN_CONV - 1:FFN_CONV, cs] * u_ref[c % 2, 8:8 + tm, hs]
            for k in range(1, FFN_CONV):
                y = y + cw_ref[FFN_CONV - 1 - k:FFN_CONV - k, cs] * u_ref[c % 2, 8 - k:8 - k + tm, hs]
            ys.append(y)
        gate, val = ys
        act_ref[:, c * fc:(c + 1) * fc] = (gate * _sigmoid(gate) * val).astype(BF16)
    out = x + _dot(act_ref[...], wd_ref[...])
    if final_norm:
        out = _rms(out, fg_ref[...])
    o_ref[...] = out


def _ffn(x2, yf, ys, yd, wo_all, g, wu_all, cw, cb, wd_all, fg, layer, seq, final_norm):
    t = x2.shape[0]
    tm = TM_PROJ
    tok = lambda n: pl.BlockSpec((tm, n), lambda i: (i, 0))
    row = lambda n: pl.BlockSpec((1, n), lambda i: (0, 0))
    return pl.pallas_call(
        functools.partial(_ffn_kernel, tm=tm, tiles_per_seq=seq // tm, final_norm=final_norm),
        grid=(t // tm,),
        in_specs=[
            tok(D_MODEL), tok(D_FOX), tok(D_SB), tok(D_SSD),
            _layer_weight(D_FOX + D_SB + D_SSD, D_MODEL, layer),
            row(D_MODEL),
            _layer_weight(D_MODEL, 2 * D_FF, layer),
            pl.BlockSpec((FFN_CONV, 2 * D_FF), lambda i: (0, 0)),
            row(2 * D_FF),
            _layer_weight(D_FF, D_MODEL, layer),
            row(D_MODEL),
        ],
        out_specs=tok(D_MODEL),
        out_shape=jax.ShapeDtypeStruct((t, D_MODEL), F32),
        scratch_shapes=[
            pltpu.VMEM((8, 2 * D_FF), F32),
            pltpu.VMEM((2, tm + 8, 2 * FFN_FC), F32),
            pltpu.VMEM((tm, D_FF), BF16),
        ],
        compiler_params=_cparams(("arbitrary",)),
        name="ffn",
    )(x2, yf, ys, yd, wo_all, g, wu_all, cw, cb, wd_all, fg)


def _pad_lanes(v, offset):
    return jnp.zeros((1, LANES), F32).at[0, offset:offset + v.shape[0]].set(v.astype(F32))


def _layer(x2, b, s, layer, wm_all, ws_all, wo_all, wu_all, wd_all, mix_g, fox_f_bias, fox_out_g, sb_out_g,
           ssd_conv_w, ssd_conv_b, ssd_dt_bias, ssd_a_log, ssd_d, ssd_norm_g, ffn_g, ffn_conv_w, ffn_conv_b,
           final_g, final_norm):
    main2, small2 = _inproj(x2, mix_g[None, :], wm_all, ws_all, layer, ssd_conv_w, ssd_conv_b[None, :], s)
    main3 = main2.reshape(b, s, N_MAIN)
    small3 = small2.reshape(b, s, LANES)

    c_full, cp = _fox_gate(small3, _pad_lanes(fox_f_bias, SMALL_FF))
    y_fox, y_sb = _attention(main3, cp, c_full, fox_out_g[None, :], sb_out_g[None, :])

    e01 = (jnp.arange(2 * LANES)[:, None] % LANES == (jnp.arange(D_SSD)[None, :] // HEAD_DIM)).astype(BF16)
    y_ssd = _ssd(main3, small3, _pad_lanes(ssd_dt_bias, SMALL_DT), _pad_lanes(ssd_a_log, SMALL_DT),
                 jnp.repeat(ssd_d, HEAD_DIM)[None, :], ssd_norm_g[None, :], e01)

    t = b * s
    return _ffn(x2, y_fox.reshape(t, D_FOX), y_sb.reshape(t, D_SB), y_ssd.reshape(t, D_SSD), wo_all,
                ffn_g[None, :], wu_all, ffn_conv_w, ffn_conv_b[None, :], wd_all, final_g[None, :],
                layer, s, final_norm)


def kernel(x, mix_norm_g, w_in, fox_f_bias, fox_out_g, sb_out_g, ssd_conv_w, ssd_conv_b, ssd_dt_bias,
           ssd_a_log, ssd_d, ssd_norm_g, w_out, ffn_norm_g, w_up, ffn_conv_w, ffn_conv_b, w_down,
           final_norm_g):
    b, s, d = x.shape
    depth = w_in.shape[0]
    ff0 = 3 * D_FOX
    dt0 = N_MAIN + FOX_HEADS
    wm_all = jnp.concatenate([w_in[:, :, :ff0], w_in[:, :, ff0 + FOX_HEADS:dt0]], axis=2).astype(BF16)
    ws_all = jnp.concatenate([w_in[:, :, dt0:dt0 + SSD_HEADS], w_in[:, :, ff0:ff0 + FOX_HEADS],
                              jnp.zeros((depth, D_MODEL, LANES - SSD_HEADS - FOX_HEADS), F32)],
                             axis=2).astype(BF16)
    wo_all, wu_all, wd_all = w_out.astype(BF16), w_up.astype(BF16), w_down.astype(BF16)
    x2 = x.reshape(b * s, d)
    for l in range(depth):
        x2 = _layer(x2, b, s, l, wm_all, ws_all, wo_all, wu_all, wd_all, mix_norm_g[l], fox_f_bias[l],
                    fox_out_g[l], sb_out_g[l], ssd_conv_w[l], ssd_conv_b[l], ssd_dt_bias[l], ssd_a_log[l],
                    ssd_d[l], ssd_norm_g[l], ffn_norm_g[l], ffn_conv_w[l], ffn_conv_b[l], final_norm_g,
                    l == depth - 1)
    return x2.reshape(b, s, d)
```
